```python
import jax, jax.numpy as jnp
from jax import lax
import numpy as np

D_MODEL = 2048
BATCH = 2
SEQ = 8192
DEPTH = 2

EPS = 1e-6
N_MEM = 256
D_MIX = D_MODEL
D_ATT = D_MIX // 2
D_RNN = D_MIX - D_ATT
N_ATT_HEADS = 8
QK_NOPE_DIM = 128
QK_ROPE_DIM = 64
QK_HEAD_DIM = QK_NOPE_DIM + QK_ROPE_DIM
V_HEAD_DIM = D_ATT // N_ATT_HEADS
Q_LORA_RANK = 512
KV_LORA_RANK = 256
ROPE_BASE = 10000.0
Q_BLOCK = 128
N_RNN_BLOCKS = 8
RNN_BLOCK = D_RNN // N_RNN_BLOCKS
CONV_WIDTH = 4
LRU_C = 8.0
N_MEM_HEADS = 4
MEM_HEAD_DIM = 128
D_MEM_ATT = N_MEM_HEADS * MEM_HEAD_DIM
N_GROUPS = 4
EXPERTS_PER_GROUP = 8
N_EXPERTS = N_GROUPS * EXPERTS_PER_GROUP
TOP_K = 2
D_EXPERT = 512
MOE_BLOCK = 128
D_IN_PROJ = Q_LORA_RANK + KV_LORA_RANK + QK_ROPE_DIM + 2 * D_RNN
IN_SPLITS = (Q_LORA_RANK,
             Q_LORA_RANK + KV_LORA_RANK,
             Q_LORA_RANK + KV_LORA_RANK + QK_ROPE_DIM,
             Q_LORA_RANK + KV_LORA_RANK + QK_ROPE_DIM + D_RNN)

kernel_name = 'hybrid_mla_rglru_memxattn_hmoe'


def rms_norm(x, g):
    xf = x.astype(jnp.float32)
    y = xf * lax.rsqrt(jnp.mean(xf * xf, axis=-1, keepdims=True) + EPS)
    return (y * g.astype(jnp.float32)).astype(x.dtype)


def rope_tables(seq):
    inv = 1.0 / (ROPE_BASE ** (jnp.arange(0, QK_ROPE_DIM, 2, dtype=jnp.float32) / QK_ROPE_DIM))
    ang = jnp.arange(seq, dtype=jnp.float32)[:, None] * inv[None, :]
    return jnp.cos(ang), jnp.sin(ang)


def apply_rope(x, cos, sin):
    x1, x2 = jnp.split(x, 2, axis=-1)
    c = cos[None, :, None, :]
    s = sin[None, :, None, :]
    return jnp.concatenate([x1 * c - x2 * s, x1 * s + x2 * c], axis=-1).astype(x.dtype)


def causal_block_attention(q, k, v):
    b, s, h, dq = q.shape
    nb = s // Q_BLOCK
    scale = dq ** -0.5
    qb = q.reshape(b, nb, Q_BLOCK, h, dq).transpose(1, 0, 2, 3, 4)
    kpos = jnp.arange(s)

    def one_block(args):
        qi, i = args
        sc = jnp.einsum('bqhd,bkhd->bhqk', qi, k, preferred_element_type=jnp.float32) * scale
        qpos = i * Q_BLOCK + jnp.arange(Q_BLOCK)
        mask = kpos[None, :] <= qpos[:, None]
        sc = jnp.where(mask[None, None], sc, -jnp.inf)
        p = jax.nn.softmax(sc, axis=-1).astype(v.dtype)
        return jnp.einsum('bhqk,bkhd->bqhd', p, v)

    out = lax.map(one_block, (qb, jnp.arange(nb)))
    return out.transpose(1, 0, 2, 3, 4).reshape(b, s, h, v.shape[-1])


def mla_group(c_q, c_kv, k_pe, q_lora_g, kv_lora_g, w_uq, w_ukv, q_head_g, k_head_g, cos, sin):
    b, s, _ = c_q.shape
    q = (rms_norm(c_q, q_lora_g) @ w_uq).reshape(b, s, N_ATT_HEADS, QK_HEAD_DIM)
    kv = (rms_norm(c_kv, kv_lora_g) @ w_ukv).reshape(b, s, N_ATT_HEADS, QK_NOPE_DIM + V_HEAD_DIM)
    k_nope, v = kv[..., :QK_NOPE_DIM], kv[..., QK_NOPE_DIM:]
    k_rope = jnp.broadcast_to(k_pe[:, :, None, :], (b, s, N_ATT_HEADS, QK_ROPE_DIM))
    k = jnp.concatenate([k_nope, k_rope], axis=-1)
    q = rms_norm(q, q_head_g)
    k = rms_norm(k, k_head_g)
    q = jnp.concatenate([q[..., :QK_NOPE_DIM], apply_rope(q[..., QK_NOPE_DIM:], cos, sin)], axis=-1)
    k = jnp.concatenate([k[..., :QK_NOPE_DIM], apply_rope(k[..., QK_NOPE_DIM:], cos, sin)], axis=-1)
    return causal_block_attention(q, k, v).reshape(b, s, D_ATT)


def rglru_group(xr, gate, conv_w, conv_b, w_rg, b_rg, w_ig, b_ig, lam):
    b, s, _ = xr.shape
    xp = jnp.pad(xr, ((0, 0), (CONV_WIDTH - 1, 0), (0, 0)))
    xc = conv_b
    for j in range(CONV_WIDTH):
        xc = xc + xp[:, j:j + s] * conv_w[j]
    xblk = xc.reshape(b, s, N_RNN_BLOCKS, RNN_BLOCK)
    r = jax.nn.sigmoid(jnp.einsum('bsnc,ncd->bsnd', xblk, w_rg).reshape(b, s, D_RNN) + b_rg)
    i = jax.nn.sigmoid(jnp.einsum('bsnc,ncd->bsnd', xblk, w_ig).reshape(b, s, D_RNN) + b_ig)
    log_a = (-LRU_C * r.astype(jnp.float32)) * jax.nn.softplus(-lam.astype(jnp.float32))
    a = jnp.exp(log_a)
    u = jnp.sqrt(-jnp.expm1(2.0 * log_a)) * (i * xc).astype(jnp.float32)

    def combine(left, right):
        a1, b1 = left
        a2, b2 = right
        return a1 * a2, a2 * b1 + b2

    _, h = lax.associative_scan(combine, (a, u), axis=1)
    return h.astype(xr.dtype) * jax.nn.gelu(gate)


def memory_xattn(x, mem, norm_g, mem_norm_g, w_q, w_k, w_v, q_g, k_g, w_o):
    b, s, _ = x.shape
    m_len = mem.shape[1]
    h = rms_norm(x, norm_g)
    m = rms_norm(mem, mem_norm_g)
    q = rms_norm((h @ w_q).reshape(b, s, N_MEM_HEADS, MEM_HEAD_DIM), q_g)
    k = rms_norm((m @ w_k).reshape(b, m_len, N_MEM_HEADS, MEM_HEAD_DIM), k_g)
    v = (m @ w_v).reshape(b, m_len, N_MEM_HEADS, MEM_HEAD_DIM)
    sc = jnp.einsum('bshd,bmhd->bhsm', q, k, preferred_element_type=jnp.float32) * (MEM_HEAD_DIM ** -0.5)
    p = jax.nn.softmax(sc, axis=-1).astype(v.dtype)
    o = jnp.einsum('bhsm,bmhd->bshd', p, v).reshape(b, s, D_MEM_ATT)
    return o @ w_o


def hierarchical_moe(x, norm_g, w_rgrp, b_rgrp, w_rexp, b_rexp, w_gate, w_up, w_down):
    b, s, d = x.shape
    t = b * s
    h = rms_norm(x, norm_g).reshape(t, d)
    g_prob = jax.nn.softmax((h @ w_rgrp).astype(jnp.float32) + b_rgrp, axis=-1)
    g_w, g_idx = lax.top_k(g_prob, 1)
    e_logits = ((h @ w_rexp).astype(jnp.float32) + b_rexp).reshape(t, N_GROUPS, EXPERTS_PER_GROUP)
    sel = jnp.broadcast_to(g_idx[:, :, None], (t, 1, EXPERTS_PER_GROUP))
    e_prob = jax.nn.softmax(jnp.take_along_axis(e_logits, sel, axis=1)[:, 0], axis=-1)
    e_w, e_loc = lax.top_k(e_prob, TOP_K)
    e_w = e_w / jnp.sum(e_w, axis=-1, keepdims=True)
    weights = g_w * e_w
    eid = g_idx * EXPERTS_PER_GROUP + e_loc
    n_asg = t * TOP_K
    eid_f = eid.reshape(n_asg)
    w_f = weights.reshape(n_asg)
    tok_f = jnp.repeat(jnp.arange(t, dtype=jnp.int32), TOP_K)
    order = jnp.argsort(eid_f)
    e_sorted = eid_f[order]
    counts = jnp.bincount(eid_f, length=N_EXPERTS)
    padded = (counts + MOE_BLOCK - 1) // MOE_BLOCK * MOE_BLOCK
    pad_end = jnp.cumsum(padded)
    pad_start = pad_end - padded
    start = jnp.cumsum(counts) - counts
    dest = pad_start[e_sorted] + jnp.arange(n_asg) - start[e_sorted]
    n_rows = ((n_asg + MOE_BLOCK - 1) // MOE_BLOCK + N_EXPERTS) * MOE_BLOCK
    n_blk = n_rows // MOE_BLOCK
    buf_tok = jnp.zeros((n_rows,), jnp.int32).at[dest].set(tok_f[order])
    buf_w = jnp.zeros((n_rows,), jnp.float32).at[dest].set(w_f[order])
    blk_e = jnp.minimum(jnp.searchsorted(pad_end, jnp.arange(n_blk) * MOE_BLOCK, side='right'), N_EXPERTS - 1)
    xb = h[buf_tok].reshape(n_blk, MOE_BLOCK, d)

    def expert_block(args):
        xi, e = args
        return (jax.nn.silu(xi @ w_gate[e]) * (xi @ w_up[e])) @ w_down[e]

    yb = lax.map(expert_block, (xb, blk_e)).reshape(n_rows, d)
    y = jax.ops.segment_sum(yb * buf_w[:, None].astype(yb.dtype), buf_tok, num_segments=t)
    return y.reshape(b, s, d)


def _normal(k, shape, scale):
    return jax.random.normal(k, shape, jnp.float32) * scale


def _gain(k, shape):
    return 1.0 + 0.02 * jax.random.normal(k, shape, jnp.float32)


def setup_inputs(seed: int = 0) -> dict:
    key = jax.random.key(seed)
    ks = jax.random.split(key, 40)
    L = DEPTH
    u = jax.random.uniform(ks[20], (L, D_RNN), jnp.float32, 0.9, 0.999)
    a0 = u ** (1.0 / LRU_C)
    lam = jnp.log(a0) - jnp.log1p(-a0)
    return {
        'x': _normal(ks[0], (BATCH, SEQ, D_MODEL), 1.0),
        'mem': _normal(ks[1], (BATCH, N_MEM, D_MODEL), 1.0),
        'mix_norm_g': _gain(ks[2], (L, D_MODEL)),
        'w_in': _normal(ks[3], (L, D_MODEL, D_IN_PROJ), D_MODEL ** -0.5),
        'q_lora_norm_g': _gain(ks[4], (L, Q_LORA_RANK)),
        'kv_lora_norm_g': _gain(ks[5], (L, KV_LORA_RANK)),
        'w_uq': _normal(ks[6], (L, Q_LORA_RANK, N_ATT_HEADS * QK_HEAD_DIM), Q_LORA_RANK ** -0.5),
        'w_ukv': _normal(ks[7], (L, KV_LORA_RANK, N_ATT_HEADS * (QK_NOPE_DIM + V_HEAD_DIM)), KV_LORA_RANK ** -0.5),
        'att_q_norm_g': _gain(ks[8], (L, QK_HEAD_DIM)),
        'att_k_norm_g': _gain(ks[9], (L, QK_HEAD_DIM)),
        'conv_w': _normal(ks[10], (L, CONV_WIDTH, D_RNN), CONV_WIDTH ** -0.5),
        'conv_b': _normal(ks[11], (L, D_RNN), 0.01),
        'w_rgate': _normal(ks[12], (L, N_RNN_BLOCKS, RNN_BLOCK, RNN_BLOCK), RNN_BLOCK ** -0.5),
        'b_rgate': _normal(ks[13], (L, D_RNN), 0.01),
        'w_igate': _normal(ks[14], (L, N_RNN_BLOCKS, RNN_BLOCK, RNN_BLOCK), RNN_BLOCK ** -0.5),
        'b_igate': _normal(ks[15], (L, D_RNN), 0.01),
        'lru_lambda': lam,
        'att_out_norm_g': _gain(ks[16], (L, D_ATT)),
        'rnn_out_norm_g': _gain(ks[17], (L, D_RNN)),
        'w_out': _normal(ks[18], (L, D_MIX, D_MODEL), D_MIX ** -0.5),
        'xattn_norm_g': _gain(ks[19], (L, D_MODEL)),
        'mem_norm_g': _gain(ks[21], (L, D_MODEL)),
        'w_mq': _normal(ks[22], (L, D_MODEL, D_MEM_ATT), D_MODEL ** -0.5),
        'w_mk': _normal(ks[23], (L, D_MODEL, D_MEM_ATT), D_MODEL ** -0.5),
        'w_mv': _normal(ks[24], (L, D_MODEL, D_MEM_ATT), D_MODEL ** -0.5),
        'mem_q_norm_g': _gain(ks[25], (L, MEM_HEAD_DIM)),
        'mem_k_norm_g': _gain(ks[26], (L, MEM_HEAD_DIM)),
        'w_mo': _normal(ks[27], (L, D_MEM_ATT, D_MODEL), D_MEM_ATT ** -0.5),
        'moe_norm_g': _gain(ks[28], (L, D_MODEL)),
        'w_router_group': _normal(ks[29], (L, D_MODEL, N_GROUPS), D_MODEL ** -0.5),
        'b_router_group': _normal(ks[30], (L, N_GROUPS), 0.01),
        'w_router_expert': _normal(ks[31], (L, D_MODEL, N_EXPERTS), D_MODEL ** -0.5),
        'b_router_expert': _normal(ks[32], (L, N_EXPERTS), 0.01),
        'w_exp_gate': _normal(ks[33], (L, N_EXPERTS, D_MODEL, D_EXPERT), D_MODEL ** -0.5),
        'w_exp_up': _normal(ks[34], (L, N_EXPERTS, D_MODEL, D_EXPERT), D_MODEL ** -0.5),
        'w_exp_down': _normal(ks[35], (L, N_EXPERTS, D_EXPERT, D_MODEL), D_EXPERT ** -0.5),
    }


def reference(x, mem, mix_norm_g, w_in, q_lora_norm_g, kv_lora_norm_g, w_uq, w_ukv,
              att_q_norm_g, att_k_norm_g, conv_w, conv_b, w_rgate, b_rgate, w_igate, b_igate,
              lru_lambda, att_out_norm_g, rnn_out_norm_g, w_out, xattn_norm_g, mem_norm_g,
              w_mq, w_mk, w_mv, mem_q_norm_g, mem_k_norm_g, w_mo, moe_norm_g,
              w_router_group, b_router_group, w_router_expert, b_router_expert,
              w_exp_gate, w_exp_up, w_exp_down):
    cos, sin = rope_tables(x.shape[1])
    for l in range(DEPTH):
        z = rms_norm(x, mix_norm_g[l]) @ w_in[l]
        c_q, c_kv, k_pe, xr, gate = jnp.split(z, IN_SPLITS, axis=-1)
        k_pe = apply_rope(rms_norm(k_pe[:, :, None, :], att_k_norm_g[l][QK_NOPE_DIM:]), cos, sin)[:, :, 0, :] * 0 + k_pe if False else k_pe
        y_att = mla_group(c_q, c_kv, k_pe, q_lora_norm_g[l], kv_lora_norm_g[l], w_uq[l], w_ukv[l],
                          att_q_norm_g[l], att_k_norm_g[l], cos, sin)
        y_rnn = rglru_group(xr, gate, conv_w[l], conv_b[l], w_rgate[l], b_rgate[l],
                            w_igate[l], b_igate[l], lru_lambda[l])
        y_mix = jnp.concatenate([rms_norm(y_att, att_out_norm_g[l]),
                                 rms_norm(y_rnn, rnn_out_norm_g[l])], axis=-1)
        x = x + y_mix @ w_out[l]
        x = x + memory_xattn(x, mem, xattn_norm_g[l], mem_norm_g[l], w_mq[l], w_mk[l], w_mv[l],
                             mem_q_norm_g[l], mem_k_norm_g[l], w_mo[l])
        x = x + hierarchical_moe(x, moe_norm_g[l], w_router_group[l], b_router_group[l],
                                 w_router_expert[l], b_router_expert[l],
                                 w_exp_gate[l], w_exp_up[l], w_exp_down[l])
    return x
```

```python
import functools

import jax
import jax.numpy as jnp
from jax import lax
from jax.experimental import pallas as pl
from jax.experimental.pallas import tpu as pltpu

EPS = 1e-6
N_ATT_HEADS = 8
QK_NOPE_DIM = 128
QK_ROPE_DIM = 64
QK_HEAD_DIM = QK_NOPE_DIM + QK_ROPE_DIM
Q_LORA_RANK = 512
KV_LORA_RANK = 256
ROPE_BASE = 10000.0
N_RNN_BLOCKS = 8
CONV_WIDTH = 4
LRU_C = 8.0
N_MEM_HEADS = 4
MEM_HEAD_DIM = 128
N_GROUPS = 4
EXPERTS_PER_GROUP = 8
N_EXPERTS = N_GROUPS * EXPERTS_PER_GROUP

LANE = 128
HEAD_SLOT = 2 * LANE
NEG_BIG = -1e30

VMEM_LIMIT = 56 * 1024 * 1024

F32 = jnp.float32
BF16 = jnp.bfloat16


def _cparams(sem):
    return pltpu.CompilerParams(dimension_semantics=sem, vmem_limit_bytes=VMEM_LIMIT)


def _rms(x, g):
    return x * lax.rsqrt(jnp.mean(x * x, axis=-1, keepdims=True) + EPS) * g


def _rope_slot(r, cos_t, sin_a, sin_b):
    return r * cos_t + pltpu.roll(r, 96, 1) * sin_a + pltpu.roll(r, 32, 1) * sin_b


def _inproj_kernel(x_ref, gmix_ref, win_ref, gql_ref, gkvl_ref, wuq_ref, wuk_ref, wuv_ref,
                   gq_ref, gk_ref, cos_ref, sina_ref, sinb_ref,
                   q_ref, k_ref, v_ref, xr_ref, gate_ref):
    d_rnn = xr_ref.shape[-1]
    h = _rms(x_ref[...], gmix_ref[...]).astype(BF16)
    z = jnp.dot(h, win_ref[...], preferred_element_type=F32)
    o_kv = Q_LORA_RANK
    o_pe = o_kv + KV_LORA_RANK
    o_xr = o_pe + LANE
    o_gate = o_xr + d_rnn
    xr_ref[...] = z[:, o_xr:o_gate]
    gate_ref[...] = z[:, o_gate:o_gate + d_rnn]

    cos_t = cos_ref[...]
    sin_a = sina_ref[...]
    sin_b = sinb_ref[...]
    inv_d = 1.0 / QK_HEAD_DIM
    scale = QK_HEAD_DIM ** -0.5

    cq = _rms(z[:, :o_kv], gql_ref[...]).astype(BF16)
    qf = jnp.dot(cq, wuq_ref[...], preferred_element_type=F32)
    gq = gq_ref[...]
    for hd in range(N_ATT_HEADS):
        slot = qf[:, hd * HEAD_SLOT:(hd + 1) * HEAD_SLOT]
        ss = jnp.sum(slot * slot, axis=-1, keepdims=True) * inv_d
        sn = slot * lax.rsqrt(ss + EPS) * gq
        q_ref[:, hd * HEAD_SLOT:hd * HEAD_SLOT + LANE] = (sn[:, :LANE] * scale).astype(BF16)
        q_ref[:, hd * HEAD_SLOT + LANE:(hd + 1) * HEAD_SLOT] = (
            _rope_slot(sn[:, LANE:], cos_t, sin_a, sin_b) * scale).astype(BF16)

    ckv = _rms(z[:, o_kv:o_pe], gkvl_ref[...]).astype(BF16)
    kn = jnp.dot(ckv, wuk_ref[...], preferred_element_type=F32)
    v_ref[...] = jnp.dot(ckv, wuv_ref[...], preferred_element_type=F32).astype(BF16)
    kpe = z[:, o_pe:o_xr]
    pe_ss = jnp.sum(kpe * kpe, axis=-1, keepdims=True)
    gk = gk_ref[...]
    for hd in range(N_ATT_HEADS):
        kh = kn[:, hd * LANE:(hd + 1) * LANE]
        ss = (jnp.sum(kh * kh, axis=-1, keepdims=True) + pe_ss) * inv_d
        inv = lax.rsqrt(ss + EPS)
        k_ref[:, hd * HEAD_SLOT:hd * HEAD_SLOT + LANE] = (kh * inv * gk[:, :LANE]).astype(BF16)
        k_ref[:, hd * HEAD_SLOT + LANE:(hd + 1) * HEAD_SLOT] = _rope_slot(
            kpe * inv * gk[:, LANE:], cos_t, sin_a, sin_b).astype(BF16)


def _inproj(x2d, gmix, win, gql, gkvl, wuq, wuk, wuv, gq, gk, cos_t, sin_a, sin_b, seq, tm):
    t, d = x2d.shape
    d_rnn = (win.shape[1] - Q_LORA_RANK - KV_LORA_RANK - LANE) // 2
    n_s = seq // tm
    const = lambda i: (0, 0)
    tile = lambda i: (i, 0)
    pos = lambda i: (i % n_s, 0)
    full = lambda a: pl.BlockSpec(a.shape, const)
    return pl.pallas_call(
        _inproj_kernel,
        grid=(t // tm,),
        in_specs=[pl.BlockSpec((tm, d), tile), full(gmix), full(win), full(gql), full(gkvl),
                  full(wuq), full(wuk), full(wuv), full(gq), full(gk),
                  pl.BlockSpec((tm, LANE), pos), pl.BlockSpec((tm, LANE), pos),
                  pl.BlockSpec((tm, LANE), pos)],
        out_specs=[pl.BlockSpec((tm, N_ATT_HEADS * HEAD_SLOT), tile),
                   pl.BlockSpec((tm, N_ATT_HEADS * HEAD_SLOT), tile),
                   pl.BlockSpec((tm, N_ATT_HEADS * LANE), tile),
                   pl.BlockSpec((tm, d_rnn), tile), pl.BlockSpec((tm, d_rnn), tile)],
        out_shape=[jax.ShapeDtypeStruct((t, N_ATT_HEADS * HEAD_SLOT), BF16),
                   jax.ShapeDtypeStruct((t, N_ATT_HEADS * HEAD_SLOT), BF16),
                   jax.ShapeDtypeStruct((t, N_ATT_HEADS * LANE), BF16),
                   jax.ShapeDtypeStruct((t, d_rnn), F32),
                   jax.ShapeDtypeStruct((t, d_rnn), F32)],
        compiler_params=_cparams(("parallel",)),
        name="inproj",
    )(x2d, gmix, win, gql, gkvl, wuq, wuk, wuv, gq, gk, cos_t, sin_a, sin_b)


def _attn_kernel(q_ref, k_ref, v_ref, o_ref, m_sc, l_sc, acc_sc, *, tq, tk):
    i = pl.program_id(2)
    q = q_ref[...]
    m_sc[...] = jnp.full(m_sc.shape, NEG_BIG, F32)
    l_sc[...] = jnp.zeros(l_sc.shape, F32)
    acc_sc[...] = jnp.zeros(acc_sc.shape, F32)

    def step(j, masked):
        start = pl.multiple_of(j * tk, tk)
        kb = k_ref[pl.ds(start, tk), :]
        vb = v_ref[pl.ds(start, tk), :]
        s = lax.dot_general(q, kb, (((1,), (1,)), ((), ())), preferred_element_type=F32)
        if masked:
            qpos = i * tq + lax.broadcasted_iota(jnp.int32, (tq, tk), 0)
            kpos = j * tk + lax.broadcasted_iota(jnp.int32, (tq, tk), 1)
            s = jnp.where(kpos <= qpos, s, NEG_BIG)
        m_prev = m_sc[...]
        m_new = jnp.maximum(m_prev, jnp.max(s, axis=-1, keepdims=True))
        p = jnp.exp(s - m_new)
        alpha = jnp.exp(m_prev - m_new)
        l_sc[...] = alpha * l_sc[...] + jnp.sum(p, axis=-1, keepdims=True)
        acc_sc[...] = alpha * acc_sc[...] + jnp.dot(p.astype(BF16), vb, preferred_element_type=F32)
        m_sc[...] = m_new

    n_diag = tq // tk
    n_full = i * n_diag

    def body(j, c):
        step(j, False)
        return c

    lax.fori_loop(0, n_full, body, 0)
    for dj in range(n_diag):
        step(n_full + dj, True)
    o_ref[...] = acc_sc[...] / l_sc[...]


def _attention(q, k, v, tq, tk):
    b, s, _ = q.shape
    kern = functools.partial(_attn_kernel, tq=tq, tk=tk)
    return pl.pallas_call(
        kern,
        grid=(b, N_ATT_HEADS, s // tq),
        in_specs=[pl.BlockSpec((None, tq, HEAD_SLOT), lambda bi, h, i: (bi, i, h)),
                  pl.BlockSpec((None, s, HEAD_SLOT), lambda bi, h, i: (bi, 0, h)),
                  pl.BlockSpec((None, s, LANE), lambda bi, h, i: (bi, 0, h))],
        out_specs=pl.BlockSpec((None, tq, LANE), lambda bi, h, i: (bi, i, h)),
        out_shape=jax.ShapeDtypeStruct((b, s, N_ATT_HEADS * LANE), F32),
        scratch_shapes=[pltpu.VMEM((tq, 1), F32), pltpu.VMEM((tq, 1), F32),
                        pltpu.VMEM((tq, LANE), F32)],
        compiler_params=_cparams(("parallel", "parallel", "arbitrary")),
        name="attn",
    )(q, k, v)


def _rglru_kernel(xr_ref, gate_ref, cw_ref, cb_ref, wr_ref, br_ref, wi_ref, bi_ref, lam_ref,
                  y_ref, tail_sc, h_sc, *, ts):
    i = pl.program_id(1)

    @pl.when(i == 0)
    def _():
        tail_sc[...] = jnp.zeros(tail_sc.shape, F32)
        h_sc[...] = jnp.zeros(h_sc.shape, F32)

    xr = xr_ref[...]
    c = xr.shape[-1]
    ext = jnp.concatenate([tail_sc[...], xr], axis=0)
    cw = cw_ref[...]
    xc = jnp.broadcast_to(cb_ref[...], xr.shape)
    for j in range(CONV_WIDTH):
        off = 8 - (CONV_WIDTH - 1) + j
        xc = xc + ext[off:off + ts, :] * cw[j:j + 1, :]
    tail_sc[...] = xr[ts - 8:, :]

    blk = c // N_RNN_BLOCKS
    xcb = xc.astype(BF16)
    r_parts = []
    i_parts = []
    for n in range(N_RNN_BLOCKS):
        xb = xcb[:, n * blk:(n + 1) * blk]
        r_parts.append(jnp.dot(xb, wr_ref[n], preferred_element_type=F32))
        i_parts.append(jnp.dot(xb, wi_ref[n], preferred_element_type=F32))
    r = jax.nn.sigmoid(jnp.concatenate(r_parts, axis=-1) + br_ref[...])
    ig = jax.nn.sigmoid(jnp.concatenate(i_parts, axis=-1) + bi_ref[...])

    nlam = -lam_ref[...]
    softplus = jnp.maximum(nlam, 0.0) + jnp.log(1.0 + jnp.exp(-jnp.abs(nlam)))
    log_a = (-LRU_C * r) * softplus
    a = jnp.exp(log_a)
    u = jnp.sqrt(1.0 - jnp.exp(2.0 * log_a)) * (ig * xc)

    row = lax.broadcasted_iota(jnp.int32, (ts, 1), 0)
    sh = 1
    while sh < ts:
        a_sh = pltpu.roll(a, sh, 0)
        u_sh = pltpu.roll(u, sh, 0)
        valid = row >= sh
        u = jnp.where(valid, a * u_sh + u, u)
        a = jnp.where(valid, a * a_sh, a)
        sh *= 2
    hcur = u + a * h_sc[0:1, :]
    h_sc[...] = jnp.broadcast_to(hcur[ts - 1:ts, :], h_sc.shape)
    y_ref[...] = hcur * jax.nn.gelu(gate_ref[...])


def _rglru(xr, gate, cw, cb, wr, br, wi, bi, lam, batch, seq, ts):
    t, c = xr.shape
    n_s = seq // ts
    tile = lambda b, i: (b * n_s + i, 0)
    c2 = lambda b, i: (0, 0)
    c3 = lambda b, i: (0, 0, 0)
    kern = functools.partial(_rglru_kernel, ts=ts)
    return pl.pallas_call(
        kern,
        grid=(batch, n_s),
        in_specs=[pl.BlockSpec((ts, c), tile), pl.BlockSpec((ts, c), tile),
                  pl.BlockSpec(cw.shape, c2), pl.BlockSpec(cb.shape, c2),
                  pl.BlockSpec(wr.shape, c3), pl.BlockSpec(br.shape, c2),
                  pl.BlockSpec(wi.shape, c3), pl.BlockSpec(bi.shape, c2),
                  pl.BlockSpec(lam.shape, c2)],
        out_specs=pl.BlockSpec((ts, c), tile),
        out_shape=jax.ShapeDtypeStruct((t, c), F32),
        scratch_shapes=[pltpu.VMEM((8, c), F32), pltpu.VMEM((8, c), F32)],
        compiler_params=_cparams(("parallel", "arbitrary")),
        name="rglru",
    )(xr, gate, cw, cb, wr, br, wi, bi, lam)


def _memkv_kernel(mem_ref, g_ref, wk_ref, wv_ref, gk_ref, k_ref, v_ref):
    m = _rms(mem_ref[...], g_ref[...]).astype(BF16)
    kf = jnp.dot(m, wk_ref[...], preferred_element_type=F32)
    v_ref[...] = jnp.dot(m, wv_ref[...], preferred_element_type=F32).astype(BF16)
    gk = gk_ref[...]
    for hd in range(N_MEM_HEADS):
        sl = slice(hd * MEM_HEAD_DIM, (hd + 1) * MEM_HEAD_DIM)
        k_ref[:, sl] = _rms(kf[:, sl], gk).astype(BF16)


def _memkv(mem, g, wk, wv, gk):
    b, m_len, d = mem.shape
    dk = wk.shape[1]
    c2 = lambda i: (0, 0)
    return pl.pallas_call(
        _memkv_kernel,
        grid=(b,),
        in_specs=[pl.BlockSpec((None, m_len, d), lambda i: (i, 0, 0)), pl.BlockSpec(g.shape, c2),
                  pl.BlockSpec(wk.shape, c2), pl.BlockSpec(wv.shape, c2), pl.BlockSpec(gk.shape, c2)],
        out_specs=[pl.BlockSpec((None, m_len, dk), lambda i: (i, 0, 0)),
                   pl.BlockSpec((None, m_len, dk), lambda i: (i, 0, 0))],
        out_shape=[jax.ShapeDtypeStruct((b, m_len, dk), BF16),
                   jax.ShapeDtypeStruct((b, m_len, dk), BF16)],
        compiler_params=_cparams(("parallel",)),
        name="memkv",
    )(mem, g, wk, wv, gk)


def _post_kernel(x_ref, ya_ref, yr_ref, ga_ref, gr_ref, woa_ref, wor_ref, gx_ref, wmq_ref, gmq_ref,
                 km_ref, vm_ref, wmo_ref, gmoe_ref, wrt_ref, brt_ref,
                 x2_ref, h2_ref, ri_ref, rw_ref, cnt_ref, cnt_sc, *, tm):
    step = pl.program_id(0)

    @pl.when(step == 0)
    def _():
        cnt_sc[...] = jnp.zeros(cnt_sc.shape, F32)

    ya = _rms(ya_ref[...], ga_ref[...]).astype(BF16)
    yr = _rms(yr_ref[...], gr_ref[...]).astype(BF16)
    x1 = (x_ref[...] + jnp.dot(ya, woa_ref[...], preferred_element_type=F32)
          + jnp.dot(yr, wor_ref[...], preferred_element_type=F32))

    hq = _rms(x1, gx_ref[...]).astype(BF16)
    qf = jnp.dot(hq, wmq_ref[...], preferred_element_type=F32)
    gmq = gmq_ref[...]
    o_parts = []
    for hd in range(N_MEM_HEADS):
        sl = slice(hd * MEM_HEAD_DIM, (hd + 1) * MEM_HEAD_DIM)
        qh = (_rms(qf[:, sl], gmq) * (MEM_HEAD_DIM ** -0.5)).astype(BF16)
        s = lax.dot_general(qh, km_ref[:, sl], (((1,), (1,)), ((), ())), preferred_element_type=F32)
        p = jnp.exp(s - jnp.max(s, axis=-1, keepdims=True))
        p = p / jnp.sum(p, axis=-1, keepdims=True)
        o_parts.append(jnp.dot(p.astype(BF16), vm_ref[:, sl], preferred_element_type=F32))
    o = jnp.concatenate(o_parts, axis=-1).astype(BF16)
    x2 = x1 + jnp.dot(o, wmo_ref[...], preferred_element_type=F32)
    x2_ref[...] = x2

    h2 = _rms(x2, gmoe_ref[...])
    h2_ref[...] = h2
    logits = jnp.dot(h2, wrt_ref[...], preferred_element_type=F32,
                     precision=lax.Precision.HIGHEST) + brt_ref[...]
    lane_i = lax.broadcasted_iota(jnp.int32, (tm, LANE), 1)
    lane = lane_i.astype(F32)
    big = float(LANE)
    is_g = lane_i < N_GROUPS
    gl = jnp.where(is_g, logits, NEG_BIG)
    gmax = jnp.max(gl, axis=-1, keepdims=True)
    gidx = jnp.min(jnp.where(gl == gmax, lane, big), axis=-1, keepdims=True)
    g_w = 1.0 / jnp.sum(jnp.where(is_g, jnp.exp(gl - gmax), 0.0), axis=-1, keepdims=True)
    lo = N_GROUPS + gidx * EXPERTS_PER_GROUP
    in_grp = (lane >= lo) & (lane < lo + EXPERTS_PER_GROUP)
    el = jnp.where(in_grp, logits, NEG_BIG)
    m1 = jnp.max(el, axis=-1, keepdims=True)
    i1 = jnp.min(jnp.where(el == m1, lane, big), axis=-1, keepdims=True)
    el2 = jnp.where(lane == i1, NEG_BIG, el)
    m2 = jnp.max(el2, axis=-1, keepdims=True)
    i2 = jnp.min(jnp.where(el2 == m2, lane, big), axis=-1, keepdims=True)
    zsum = jnp.sum(jnp.where(in_grp, jnp.exp(el - m1), 0.0), axis=-1, keepdims=True)
    p1 = 1.0 / zsum
    p2 = jnp.exp(m2 - m1) / zsum
    w1 = g_w * (p1 / (p1 + p2))
    w2 = g_w * (p2 / (p1 + p2))
    e1 = i1 - N_GROUPS
    e2 = i2 - N_GROUPS

    oh1 = lane == e1
    oh2 = lane == e2
    oh = jnp.where(oh1 | oh2, 1.0, 0.0)
    r_i = lax.broadcasted_iota(jnp.int32, (tm, tm), 0)
    c_i = lax.broadcasted_iota(jnp.int32, (tm, tm), 1)
    ltri = jnp.where(c_i < r_i, 1.0, 0.0).astype(BF16)
    cum = jnp.dot(ltri, oh.astype(BF16), preferred_element_type=F32) + cnt_sc[0:1, :]
    rank1 = jnp.sum(jnp.where(oh1, cum, 0.0), axis=-1, keepdims=True)
    rank2 = jnp.sum(jnp.where(oh2, cum, 0.0), axis=-1, keepdims=True)
    new_cnt = cnt_sc[0:1, :] + jnp.sum(oh, axis=0, keepdims=True)
    cnt_sc[...] = jnp.broadcast_to(new_cnt, cnt_sc.shape)
    cnt_ref[...] = jnp.broadcast_to(new_cnt, cnt_ref.shape)

    ri = jnp.where(lane_i == 0, e1, jnp.where(lane_i == 1, e2,
                   jnp.where(lane_i == 2, rank1, jnp.where(lane_i == 3, rank2, 0.0))))
    ri_ref[...] = ri.astype(jnp.int32)
    rw_ref[...] = jnp.where(lane_i == 0, w1, jnp.where(lane_i == 1, w2, 0.0))


def _post(x2d, ya, yr, ga, gr, woa, wor, gx, wmq, gmq, km, vm, wmo, gmoe, wrt, brt, seq, tm):
    t, d = x2d.shape
    n_s = seq // tm
    tile = lambda i: (i, 0)
    c2 = lambda i: (0, 0)
    full = lambda a: pl.BlockSpec(a.shape, c2)
    mem_spec = lambda a: pl.BlockSpec((None,) + a.shape[1:], lambda i: (i // n_s, 0, 0))
    kern = functools.partial(_post_kernel, tm=tm)
    return pl.pallas_call(
        kern,
        grid=(t // tm,),
        in_specs=[pl.BlockSpec((tm, d), tile), pl.BlockSpec((tm, ya.shape[1]), tile),
                  pl.BlockSpec((tm, yr.shape[1]), tile), full(ga), full(gr), full(woa), full(wor),
                  full(gx), full(wmq), full(gmq), mem_spec(km), mem_spec(vm), full(wmo),
                  full(gmoe), full(wrt), full(brt)],
        out_specs=[pl.BlockSpec((tm, d), tile), pl.BlockSpec((tm, d), tile),
                   pl.BlockSpec((tm, LANE), tile), pl.BlockSpec((tm, LANE), tile),
                   pl.BlockSpec((8, LANE), c2)],
        out_shape=[jax.ShapeDtypeStruct((t, d), F32), jax.ShapeDtypeStruct((t, d), F32),
                   jax.ShapeDtypeStruct((t, LANE), jnp.int32),
                   jax.ShapeDtypeStruct((t, LANE), F32),
                   jax.ShapeDtypeStruct((8, LANE), F32)],
        scratch_shapes=[pltpu.VMEM((8, LANE), F32)],
        compiler_params=_cparams(("arbitrary",)),
        name="post",
    )(x2d, ya, yr, ga, gr, woa, wor, gx, wmq, gmq, km, vm, wmo, gmoe, wrt, brt)


def _row_copy(src_hbm, dst_vmem, sem, src_row, dst_row):
    return pltpu.make_async_copy(src_hbm.at[pl.ds(src_row, 1)], dst_vmem.at[pl.ds(dst_row, 1)], sem)


def _expert_kernel(blk_e_ref, tok_ref, nused_ref, h_hbm, bw_ref, wg_ref, wu_ref, wd_ref, y_ref,
                   xbuf, wg_sc, wu_sc, wd_sc, sem, *, bm):
    i = pl.program_id(0)
    used = i < nused_ref[0]

    @pl.when(used)
    def _():
        def issue(r, c):
            _row_copy(h_hbm, xbuf, sem, tok_ref[i * bm + r], r).start()
            return c

        lax.fori_loop(0, bm, issue, 0)
        prev = jnp.maximum(i - 1, 0)

        @pl.when((i == 0) | (blk_e_ref[i] != blk_e_ref[prev]))
        def _():
            wg_sc[...] = wg_ref[...].astype(BF16)
            wu_sc[...] = wu_ref[...].astype(BF16)
            wd_sc[...] = wd_ref[...].astype(BF16)

        def drain(r, c):
            _row_copy(h_hbm, xbuf, sem, 0, r).wait()
            return c

        lax.fori_loop(0, bm, drain, 0)
        x = xbuf[...].astype(BF16)
        g = jnp.dot(x, wg_sc[...], preferred_element_type=F32)
        u = jnp.dot(x, wu_sc[...], preferred_element_type=F32)
        act = (g * jax.nn.sigmoid(g) * u).astype(BF16)
        y = jnp.dot(act, wd_sc[...], preferred_element_type=F32)
        y_ref[...] = y * bw_ref[...]

    @pl.when(jnp.logical_not(used))
    def _():
        y_ref[...] = jnp.zeros(y_ref.shape, F32)


def _experts(blk_e, buf_tok, nused, h2, buf_w, w_gate, w_up, w_down, layer, bm):
    n_rows = buf_tok.shape[0]
    d = h2.shape[1]
    de = w_gate.shape[-1]
    wmap = lambda i, be, tk, nu: (layer, be[i], 0, 0)
    kern = functools.partial(_expert_kernel, bm=bm)
    grid_spec = pltpu.PrefetchScalarGridSpec(
        num_scalar_prefetch=3,
        grid=(n_rows // bm,),
        in_specs=[pl.BlockSpec(memory_space=pl.ANY),
                  pl.BlockSpec((bm, 1), lambda i, be, tk, nu: (i, 0)),
                  pl.BlockSpec((None, None, d, de), wmap),
                  pl.BlockSpec((None, None, d, de), wmap),
                  pl.BlockSpec((None, None, de, d), wmap)],
        out_specs=pl.BlockSpec((bm, d), lambda i, be, tk, nu: (i, 0)),
        scratch_shapes=[pltpu.VMEM((bm, d), F32), pltpu.VMEM((d, de), BF16),
                        pltpu.VMEM((d, de), BF16), pltpu.VMEM((de, d), BF16),
                        pltpu.SemaphoreType.DMA(())],
    )
    return pl.pallas_call(
        kern,
        grid_spec=grid_spec,
        out_shape=jax.ShapeDtypeStruct((n_rows, d), F32),
        compiler_params=_cparams(("arbitrary",)),
        name="experts",
    )(blk_e, buf_tok, nused, h2, buf_w, w_gate, w_up, w_down)


def _combine_kernel(dest_ref, x_ref, y_hbm, o_ref, buf0, buf1, sem, *, tm):
    i = pl.program_id(0)

    def issue(r, c):
        tok = i * tm + r
        _row_copy(y_hbm, buf0, sem, dest_ref[2 * tok], r).start()
        _row_copy(y_hbm, buf1, sem, dest_ref[2 * tok + 1], r).start()
        return c

    lax.fori_loop(0, tm, issue, 0)

    def drain(r, c):
        _row_copy(y_hbm, buf0, sem, 0, r).wait()
        _row_copy(y_hbm, buf1, sem, 0, r).wait()
        return c

    lax.fori_loop(0, tm, drain, 0)
    o_ref[...] = x_ref[...] + buf0[...] + buf1[...]


def _combine(dest_flat, x2, yb, tm):
    t, d = x2.shape
    kern = functools.partial(_combine_kernel, tm=tm)
    grid_spec = pltpu.PrefetchScalarGridSpec(
        num_scalar_prefetch=1,
        grid=(t // tm,),
        in_specs=[pl.BlockSpec((tm, d), lambda i, de: (i, 0)),
                  pl.BlockSpec(memory_space=pl.ANY)],
        out_specs=pl.BlockSpec((tm, d), lambda i, de: (i, 0)),
        scratch_shapes=[pltpu.VMEM((tm, d), F32), pltpu.VMEM((tm, d), F32),
                        pltpu.SemaphoreType.DMA(())],
    )
    return pl.pallas_call(
        kern,
        grid_spec=grid_spec,
        out_shape=jax.ShapeDtypeStruct((t, d), F32),
        compiler_params=_cparams(("arbitrary",)),
        name="combine",
    )(dest_flat, x2, yb)


def _slot_gain(g):
    return jnp.concatenate([g, jnp.zeros((HEAD_SLOT - QK_HEAD_DIM,), F32)])[None, :]


def _rope_slot_tables(seq):
    inv = 1.0 / (ROPE_BASE ** (jnp.arange(0, QK_ROPE_DIM, 2, dtype=F32) / QK_ROPE_DIM))
    ang = jnp.arange(seq, dtype=F32)[:, None] * inv[None, :]
    cos, sin = jnp.cos(ang), jnp.sin(ang)
    zero = jnp.zeros_like(cos)
    cos_t = jnp.concatenate([cos, cos, zero, zero], axis=-1)
    sin_a = jnp.concatenate([-sin, zero, zero, zero], axis=-1)
    sin_b = jnp.concatenate([zero, sin, zero, zero], axis=-1)
    return cos_t, sin_a, sin_b


def _pick_tile(n, pref):
    t = min(n, pref)
    while n % t:
        t //= 2
    return t


def kernel(x, mem, mix_norm_g, w_in, q_lora_norm_g, kv_lora_norm_g, w_uq, w_ukv, att_q_norm_g, att_k_norm_g, conv_w, conv_b, w_rgate, b_rgate, w_igate, b_igate, lru_lambda, att_out_norm_g, rnn_out_norm_g, w_out, xattn_norm_g, mem_norm_g, w_mq, w_mk, w_mv, mem_q_norm_g, mem_k_norm_g, w_mo, moe_norm_g, w_router_group, b_router_group, w_router_expert, b_router_expert, w_exp_gate, w_exp_up, w_exp_down):
    batch, seq, d = x.shape
    depth = w_in.shape[0]
    t = batch * seq
    d_att = N_ATT_HEADS * LANE
    d_rnn = conv_w.shape[-1]
    tm = _pick_tile(seq, 256)
    tq = _pick_tile(seq, 512)
    bm = 256
    n_asg = 2 * t
    n_rows = (-(-n_asg // bm) + N_EXPERTS) * bm
    n_blk = n_rows // bm

    cos_t, sin_a, sin_b = _rope_slot_tables(seq)
    row = lambda a: a[None, :]
    xc = x.reshape(t, d)

    for l in range(depth):
        wl = w_in[l]
        o_pe = Q_LORA_RANK + KV_LORA_RANK
        win = jnp.concatenate([wl[:, :o_pe + QK_ROPE_DIM],
                               jnp.zeros((d, LANE - QK_ROPE_DIM), F32),
                               wl[:, o_pe + QK_ROPE_DIM:]], axis=1).astype(BF16)
        wuq = jnp.pad(w_uq[l].reshape(Q_LORA_RANK, N_ATT_HEADS, QK_HEAD_DIM),
                      ((0, 0), (0, 0), (0, HEAD_SLOT - QK_HEAD_DIM))
                      ).reshape(Q_LORA_RANK, N_ATT_HEADS * HEAD_SLOT).astype(BF16)
        wukv = w_ukv[l].reshape(KV_LORA_RANK, N_ATT_HEADS, 2 * LANE)
        wuk = wukv[:, :, :LANE].reshape(KV_LORA_RANK, d_att).astype(BF16)
        wuv = wukv[:, :, LANE:].reshape(KV_LORA_RANK, d_att).astype(BF16)

        q, k, v, xr, gate = _inproj(
            xc, row(mix_norm_g[l]), win, row(q_lora_norm_g[l]), row(kv_lora_norm_g[l]),
            wuq, wuk, wuv, _slot_gain(att_q_norm_g[l]), _slot_gain(att_k_norm_g[l]),
            cos_t, sin_a, sin_b, seq, tm)

        y_att = _attention(q.reshape(batch, seq, -1), k.reshape(batch, seq, -1),
                           v.reshape(batch, seq, -1), tq, tq).reshape(t, d_att)

        y_rnn = _rglru(xr, gate, conv_w[l], row(conv_b[l]), w_rgate[l].astype(BF16),
                       row(b_rgate[l]), w_igate[l].astype(BF16), row(b_igate[l]),
                       row(lru_lambda[l]), batch, seq, tm)

        km, vm = _memkv(mem, row(mem_norm_g[l]), w_mk[l].astype(BF16), w_mv[l].astype(BF16),
                        row(mem_k_norm_g[l]))

        wrt = jnp.concatenate([w_router_group[l], w_router_expert[l],
                               jnp.zeros((d, LANE - N_GROUPS - N_EXPERTS), F32)], axis=1)
        brt = jnp.concatenate([b_router_group[l], b_router_expert[l],
                               jnp.zeros((LANE - N_GROUPS - N_EXPERTS,), F32)])[None, :]
        wo = w_out[l].astype(BF16)
        x2, h2, ri, rw, cnt = _post(
            xc, y_att, y_rnn, row(att_out_norm_g[l]), row(rnn_out_norm_g[l]),
            wo[:d_att], wo[d_att:], row(xattn_norm_g[l]), w_mq[l].astype(BF16),
            row(mem_q_norm_g[l]), km, vm, w_mo[l].astype(BF16), row(moe_norm_g[l]),
            wrt, brt, seq, tm)

        counts = cnt[0, :N_EXPERTS].astype(jnp.int32)
        padded = (counts + bm - 1) // bm * bm
        pad_end = jnp.cumsum(padded)
        pad_start = pad_end - padded
        eid = ri[:, 0:2]
        dest = (pad_start[eid] + ri[:, 2:4]).reshape(n_asg)
        tok = jnp.repeat(jnp.arange(t, dtype=jnp.int32), 2)
        buf_tok = jnp.zeros((n_rows,), jnp.int32).at[dest].set(tok)
        buf_w = jnp.zeros((n_rows,), F32).at[dest].set(rw[:, 0:2].reshape(n_asg))
        blk_e = jnp.minimum(jnp.searchsorted(pad_end, jnp.arange(n_blk, dtype=jnp.int32) * bm,
                                             side='right'), N_EXPERTS - 1).astype(jnp.int32)
        nused = (pad_end[-1:] // bm).astype(jnp.int32)

        yb = _experts(blk_e, buf_tok, nused, h2, buf_w[:, None], w_exp_gate, w_exp_up,
                      w_exp_down, l, bm)
        xc = _combine(dest, x2, yb, tm)

    return xc.reshape(batch, seq, d)
```

```python
import functools

import jax
import jax.numpy as jnp
from jax import lax
from jax.experimental import pallas as pl
from jax.experimental.pallas import tpu as pltpu

EPS = 1e-6
N_ATT_HEADS = 8
QK_NOPE_DIM = 128
QK_ROPE_DIM = 64
QK_HEAD_DIM = QK_NOPE_DIM + QK_ROPE_DIM
Q_LORA_RANK = 512
KV_LORA_RANK = 256
ROPE_BASE = 10000.0
N_RNN_BLOCKS = 8
CONV_WIDTH = 4
LRU_C = 8.0
N_MEM_HEADS = 4
MEM_HEAD_DIM = 128
N_GROUPS = 4
EXPERTS_PER_GROUP = 8
N_EXPERTS = N_GROUPS * EXPERTS_PER_GROUP

LANE = 128
HEAD_SLOT = 2 * LANE
NEG_BIG = -1e30
LOG2_E = 1.4426950408889634

VMEM_LIMIT = 56 * 1024 * 1024

F32 = jnp.float32
BF16 = jnp.bfloat16


def _cparams(sem):
    return pltpu.CompilerParams(dimension_semantics=sem, vmem_limit_bytes=VMEM_LIMIT)


def _rms(x, g):
    return x * lax.rsqrt(jnp.mean(x * x, axis=-1, keepdims=True) + EPS) * g


def _rope_slot(r, cos_t, sin_a, sin_b):
    return r * cos_t + pltpu.roll(r, 96, 1) * sin_a + pltpu.roll(r, 32, 1) * sin_b


def _inproj_kernel(x_ref, gmix_ref, win_ref, gql_ref, gkvl_ref, wuq_ref, wuk_ref, wuv_ref,
                   gq_ref, gk_ref, cos_ref, sina_ref, sinb_ref,
                   q_ref, k_ref, v_ref, xr_ref, gate_ref):
    d_rnn = xr_ref.shape[-1]
    h = _rms(x_ref[...], gmix_ref[...]).astype(BF16)
    z = jnp.dot(h, win_ref[...], preferred_element_type=F32)
    o_kv = Q_LORA_RANK
    o_pe = o_kv + KV_LORA_RANK
    o_xr = o_pe + LANE
    o_gate = o_xr + d_rnn
    xr_ref[...] = z[:, o_xr:o_gate]
    gate_ref[...] = z[:, o_gate:o_gate + d_rnn]

    cos_t = cos_ref[...]
    sin_a = sina_ref[...]
    sin_b = sinb_ref[...]
    inv_d = 1.0 / QK_HEAD_DIM
    scale = LOG2_E * QK_HEAD_DIM ** -0.5

    cq = _rms(z[:, :o_kv], gql_ref[...]).astype(BF16)
    qf = jnp.dot(cq, wuq_ref[...], preferred_element_type=F32)
    gq = gq_ref[...]
    for hd in range(N_ATT_HEADS):
        slot = qf[:, hd * HEAD_SLOT:(hd + 1) * HEAD_SLOT]
        ss = jnp.sum(slot * slot, axis=-1, keepdims=True) * inv_d
        sn = slot * lax.rsqrt(ss + EPS) * gq
        q_ref[:, hd * HEAD_SLOT:hd * HEAD_SLOT + LANE] = (sn[:, :LANE] * scale).astype(BF16)
        q_ref[:, hd * HEAD_SLOT + LANE:(hd + 1) * HEAD_SLOT] = (
            _rope_slot(sn[:, LANE:], cos_t, sin_a, sin_b) * scale).astype(BF16)

    ckv = _rms(z[:, o_kv:o_pe], gkvl_ref[...]).astype(BF16)
    kn = jnp.dot(ckv, wuk_ref[...], preferred_element_type=F32)
    vv = jnp.dot(ckv, wuv_ref[...], preferred_element_type=F32).astype(BF16)
    ones = jnp.ones((vv.shape[0], LANE), BF16)
    for hd in range(N_ATT_HEADS):
        v_ref[:, hd * HEAD_SLOT:hd * HEAD_SLOT + LANE] = vv[:, hd * LANE:(hd + 1) * LANE]
        v_ref[:, hd * HEAD_SLOT + LANE:(hd + 1) * HEAD_SLOT] = ones
    kpe = z[:, o_pe:o_xr]
    pe_ss = jnp.sum(kpe * kpe, axis=-1, keepdims=True)
    gk = gk_ref[...]
    for hd in range(N_ATT_HEADS):
        kh = kn[:, hd * LANE:(hd + 1) * LANE]
        ss = (jnp.sum(kh * kh, axis=-1, keepdims=True) + pe_ss) * inv_d
        inv = lax.rsqrt(ss + EPS)
        k_ref[:, hd * HEAD_SLOT:hd * HEAD_SLOT + LANE] = (kh * inv * gk[:, :LANE]).astype(BF16)
        k_ref[:, hd * HEAD_SLOT + LANE:(hd + 1) * HEAD_SLOT] = _rope_slot(
            kpe * inv * gk[:, LANE:], cos_t, sin_a, sin_b).astype(BF16)


def _inproj(x2d, gmix, win, gql, gkvl, wuq, wuk, wuv, gq, gk, cos_t, sin_a, sin_b, seq, tm):
    t, d = x2d.shape
    d_rnn = (win.shape[1] - Q_LORA_RANK - KV_LORA_RANK - LANE) // 2
    n_s = seq // tm
    const = lambda i: (0, 0)
    tile = lambda i: (i, 0)
    pos = lambda i: (i % n_s, 0)
    full = lambda a: pl.BlockSpec(a.shape, const)
    return pl.pallas_call(
        _inproj_kernel,
        grid=(t // tm,),
        in_specs=[pl.BlockSpec((tm, d), tile), full(gmix), full(win), full(gql), full(gkvl),
                  full(wuq), full(wuk), full(wuv), full(gq), full(gk),
                  pl.BlockSpec((tm, LANE), pos), pl.BlockSpec((tm, LANE), pos),
                  pl.BlockSpec((tm, LANE), pos)],
        out_specs=[pl.BlockSpec((tm, N_ATT_HEADS * HEAD_SLOT), tile),
                   pl.BlockSpec((tm, N_ATT_HEADS * HEAD_SLOT), tile),
                   pl.BlockSpec((tm, N_ATT_HEADS * HEAD_SLOT), tile),
                   pl.BlockSpec((tm, d_rnn), tile), pl.BlockSpec((tm, d_rnn), tile)],
        out_shape=[jax.ShapeDtypeStruct((t, N_ATT_HEADS * HEAD_SLOT), BF16),
                   jax.ShapeDtypeStruct((t, N_ATT_HEADS * HEAD_SLOT), BF16),
                   jax.ShapeDtypeStruct((t, N_ATT_HEADS * HEAD_SLOT), BF16),
                   jax.ShapeDtypeStruct((t, d_rnn), F32),
                   jax.ShapeDtypeStruct((t, d_rnn), F32)],
        compiler_params=_cparams(("parallel",)),
        name="inproj",
    )(x2d, gmix, win, gql, gkvl, wuq, wuk, wuv, gq, gk, cos_t, sin_a, sin_b)


def _attn_kernel(q_ref, k_ref, v_ref, o_ref, m_sc, acc_sc, *, tc, nc):
    i = pl.program_id(2)
    m_sc[...] = jnp.full(m_sc.shape, NEG_BIG, F32)
    acc_sc[...] = jnp.zeros(acc_sc.shape, F32)

    def scores(c, j):
        start = pl.multiple_of(j * tc, tc)
        return lax.dot_general(q_ref[c * tc:(c + 1) * tc, :], k_ref[pl.ds(start, tc), :],
                               (((1,), (1,)), ((), ())), preferred_element_type=F32)

    def softmax_pv(c, s, j, masked):
        rows = slice(c * tc, (c + 1) * tc)
        start = pl.multiple_of(j * tc, tc)
        vb = v_ref[pl.ds(start, tc), :]
        if masked:
            rpos = lax.broadcasted_iota(jnp.int32, (tc, tc), 0)
            cpos = lax.broadcasted_iota(jnp.int32, (tc, tc), 1)
            s = jnp.where(cpos <= rpos, s, NEG_BIG)
        m_prev = m_sc[rows, :]
        m_new = jnp.maximum(m_prev, jnp.max(s, axis=-1, keepdims=True))
        p = jnp.exp2(s - jnp.tile(m_new, (1, tc // LANE)))
        alpha = jnp.exp2(m_prev - m_new)
        pv = jnp.dot(p.astype(BF16), vb, preferred_element_type=F32)
        acc_sc[rows, :] = jnp.tile(alpha, (1, 2)) * acc_sc[rows, :] + pv
        m_sc[rows, :] = m_new

    n_full = i * nc

    def body(j, carry):
        nxt = tuple(scores(c, j + 1) for c in range(nc))
        for c in range(nc):
            softmax_pv(c, carry[c], j, False)
        return nxt

    carry = lax.fori_loop(0, n_full, body, tuple(scores(c, 0) for c in range(nc)))
    for d in range(nc):
        for c in range(d, nc):
            s = carry[c] if d == 0 else scores(c, n_full + d)
            softmax_pv(c, s, n_full + d, c == d)
    o_ref[...] = acc_sc[:, :LANE] / acc_sc[:, LANE:]


def _attention(q, k, v, tc, nc):
    b, s, _ = q.shape
    tq = tc * nc
    kern = functools.partial(_attn_kernel, tc=tc, nc=nc)
    return pl.pallas_call(
        kern,
        grid=(b, N_ATT_HEADS, s // tq),
        in_specs=[pl.BlockSpec((None, tq, HEAD_SLOT), lambda bi, h, i: (bi, i, h)),
                  pl.BlockSpec((None, s, HEAD_SLOT), lambda bi, h, i: (bi, 0, h)),
                  pl.BlockSpec((None, s, HEAD_SLOT), lambda bi, h, i: (bi, 0, h))],
        out_specs=pl.BlockSpec((None, tq, LANE), lambda bi, h, i: (bi, i, h)),
        out_shape=jax.ShapeDtypeStruct((b, s, N_ATT_HEADS * LANE), F32),
        scratch_shapes=[pltpu.VMEM((tq, LANE), F32), pltpu.VMEM((tq, HEAD_SLOT), F32)],
        compiler_params=_cparams(("parallel", "parallel", "arbitrary")),
        name="attn",
    )(q, k, v)


def _rglru_kernel(xr_ref, gate_ref, cw_ref, cb_ref, wr_ref, br_ref, wi_ref, bi_ref, lam_ref,
                  y_ref, tail_sc, h_sc, *, ts):
    i = pl.program_id(1)

    @pl.when(i == 0)
    def _():
        tail_sc[...] = jnp.zeros(tail_sc.shape, F32)
        h_sc[...] = jnp.zeros(h_sc.shape, F32)

    xr = xr_ref[...]
    c = xr.shape[-1]
    ext = jnp.concatenate([tail_sc[...], xr], axis=0)
    cw = cw_ref[...]
    xc = jnp.broadcast_to(cb_ref[...], xr.shape)
    for j in range(CONV_WIDTH):
        off = 8 - (CONV_WIDTH - 1) + j
        xc = xc + ext[off:off + ts, :] * cw[j:j + 1, :]
    tail_sc[...] = xr[ts - 8:, :]

    blk = c // N_RNN_BLOCKS
    xcb = xc.astype(BF16)
    r_parts = []
    i_parts = []
    for n in range(N_RNN_BLOCKS):
        xb = xcb[:, n * blk:(n + 1) * blk]
        r_parts.append(jnp.dot(xb, wr_ref[n], preferred_element_type=F32))
        i_parts.append(jnp.dot(xb, wi_ref[n], preferred_element_type=F32))
    r = jax.nn.sigmoid(jnp.concatenate(r_parts, axis=-1) + br_ref[...])
    ig = jax.nn.sigmoid(jnp.concatenate(i_parts, axis=-1) + bi_ref[...])

    nlam = -lam_ref[...]
    softplus = jnp.maximum(nlam, 0.0) + jnp.log(1.0 + jnp.exp(-jnp.abs(nlam)))
    log_a = (-LRU_C * r) * softplus
    a = jnp.exp(log_a)
    u = jnp.sqrt(1.0 - jnp.exp(2.0 * log_a)) * (ig * xc)

    row = lax.broadcasted_iota(jnp.int32, (ts, 1), 0)
    sh = 1
    while sh < ts:
        a_sh = pltpu.roll(a, sh, 0)
        u_sh = pltpu.roll(u, sh, 0)
        valid = row >= sh
        u = jnp.where(valid, a * u_sh + u, u)
        a = jnp.where(valid, a * a_sh, a)
        sh *= 2
    hcur = u + a * h_sc[0:1, :]
    h_sc[...] = jnp.broadcast_to(hcur[ts - 1:ts, :], h_sc.shape)
    y_ref[...] = hcur * jax.nn.gelu(gate_ref[...])


def _rglru(xr, gate, cw, cb, wr, br, wi, bi, lam, batch, seq, ts):
    t, c = xr.shape
    n_s = seq // ts
    tile = lambda b, i: (b * n_s + i, 0)
    c2 = lambda b, i: (0, 0)
    c3 = lambda b, i: (0, 0, 0)
    kern = functools.partial(_rglru_kernel, ts=ts)
    return pl.pallas_call(
        kern,
        grid=(batch, n_s),
        in_specs=[pl.BlockSpec((ts, c), tile), pl.BlockSpec((ts, c), tile),
                  pl.BlockSpec(cw.shape, c2), pl.BlockSpec(cb.shape, c2),
                  pl.BlockSpec(wr.shape, c3), pl.BlockSpec(br.shape, c2),
                  pl.BlockSpec(wi.shape, c3), pl.BlockSpec(bi.shape, c2),
                  pl.BlockSpec(lam.shape, c2)],
        out_specs=pl.BlockSpec((ts, c), tile),
        out_shape=jax.ShapeDtypeStruct((t, c), F32),
        scratch_shapes=[pltpu.VMEM((8, c), F32), pltpu.VMEM((8, c), F32)],
        compiler_params=_cparams(("parallel", "arbitrary")),
        name="rglru",
    )(xr, gate, cw, cb, wr, br, wi, bi, lam)


def _memkv_kernel(mem_ref, g_ref, wk_ref, wv_ref, gk_ref, k_ref, v_ref):
    m = _rms(mem_ref[...], g_ref[...]).astype(BF16)
    kf = jnp.dot(m, wk_ref[...], preferred_element_type=F32)
    v_ref[...] = jnp.dot(m, wv_ref[...], preferred_element_type=F32).astype(BF16)
    gk = gk_ref[...]
    for hd in range(N_MEM_HEADS):
        sl = slice(hd * MEM_HEAD_DIM, (hd + 1) * MEM_HEAD_DIM)
        k_ref[:, sl] = _rms(kf[:, sl], gk).astype(BF16)


def _memkv(mem, g, wk, wv, gk):
    b, m_len, d = mem.shape
    dk = wk.shape[1]
    c2 = lambda i: (0, 0)
    return pl.pallas_call(
        _memkv_kernel,
        grid=(b,),
        in_specs=[pl.BlockSpec((None, m_len, d), lambda i: (i, 0, 0)), pl.BlockSpec(g.shape, c2),
                  pl.BlockSpec(wk.shape, c2), pl.BlockSpec(wv.shape, c2), pl.BlockSpec(gk.shape, c2)],
        out_specs=[pl.BlockSpec((None, m_len, dk), lambda i: (i, 0, 0)),
                   pl.BlockSpec((None, m_len, dk), lambda i: (i, 0, 0))],
        out_shape=[jax.ShapeDtypeStruct((b, m_len, dk), BF16),
                   jax.ShapeDtypeStruct((b, m_len, dk), BF16)],
        compiler_params=_cparams(("parallel",)),
        name="memkv",
    )(mem, g, wk, wv, gk)


def _post_kernel(x_ref, ya_ref, yr_ref, ga_ref, gr_ref, woa_ref, wor_ref, gx_ref, wmq_ref, gmq_ref,
                 km_ref, vm_ref, wmo_ref, gmoe_ref, wrt_ref, wrl_ref, brt_ref,
                 x2_ref, h2_ref, ri_ref, rw_ref, cnt_ref, cnt_sc, *, tm):
    step = pl.program_id(0)

    @pl.when(step == 0)
    def _():
        cnt_sc[...] = jnp.zeros(cnt_sc.shape, F32)

    ya = _rms(ya_ref[...], ga_ref[...]).astype(BF16)
    yr = _rms(yr_ref[...], gr_ref[...]).astype(BF16)
    x1 = (x_ref[...] + jnp.dot(ya, woa_ref[...], preferred_element_type=F32)
          + jnp.dot(yr, wor_ref[...], preferred_element_type=F32))

    hq = _rms(x1, gx_ref[...]).astype(BF16)
    qf = jnp.dot(hq, wmq_ref[...], preferred_element_type=F32)
    gmq = gmq_ref[...]
    o_parts = []
    for hd in range(N_MEM_HEADS):
        sl = slice(hd * MEM_HEAD_DIM, (hd + 1) * MEM_HEAD_DIM)
        qh = (_rms(qf[:, sl], gmq) * (MEM_HEAD_DIM ** -0.5)).astype(BF16)
        s = lax.dot_general(qh, km_ref[:, sl], (((1,), (1,)), ((), ())), preferred_element_type=F32)
        p = jnp.exp(s - jnp.max(s, axis=-1, keepdims=True))
        p = p / jnp.sum(p, axis=-1, keepdims=True)
        o_parts.append(jnp.dot(p.astype(BF16), vm_ref[:, sl], preferred_element_type=F32))
    o = jnp.concatenate(o_parts, axis=-1).astype(BF16)
    x2 = x1 + jnp.dot(o, wmo_ref[...], preferred_element_type=F32)
    x2_ref[...] = x2

    h2 = _rms(x2, gmoe_ref[...])
    h2_ref[...] = h2
    h_hi = h2.astype(BF16)
    h_lo = (h2 - h_hi.astype(F32)).astype(BF16)
    w_hi = wrt_ref[...]
    logits = (jnp.dot(h_hi, w_hi, preferred_element_type=F32)
              + jnp.dot(h_lo, w_hi, preferred_element_type=F32)
              + jnp.dot(h_hi, wrl_ref[...], preferred_element_type=F32)) + brt_ref[...]
    lane_i = lax.broadcasted_iota(jnp.int32, (tm, LANE), 1)
    lane = lane_i.astype(F32)
    big = float(LANE)
    is_g = lane_i < N_GROUPS
    gl = jnp.where(is_g, logits, NEG_BIG)
    gmax = jnp.max(gl, axis=-1, keepdims=True)
    gidx = jnp.min(jnp.where(gl == gmax, lane, big), axis=-1, keepdims=True)
    g_w = 1.0 / jnp.sum(jnp.where(is_g, jnp.exp(gl - gmax), 0.0), axis=-1, keepdims=True)
    lo = N_GROUPS + gidx * EXPERTS_PER_GROUP
    in_grp = (lane >= lo) & (lane < lo + EXPERTS_PER_GROUP)
    el = jnp.where(in_grp, logits, NEG_BIG)
    m1 = jnp.max(el, axis=-1, keepdims=True)
    i1 = jnp.min(jnp.where(el == m1, lane, big), axis=-1, keepdims=True)
    el2 = jnp.where(lane == i1, NEG_BIG, el)
    m2 = jnp.max(el2, axis=-1, keepdims=True)
    i2 = jnp.min(jnp.where(el2 == m2, lane, big), axis=-1, keepdims=True)
    zsum = jnp.sum(jnp.where(in_grp, jnp.exp(el - m1), 0.0), axis=-1, keepdims=True)
    p1 = 1.0 / zsum
    p2 = jnp.exp(m2 - m1) / zsum
    w1 = g_w * (p1 / (p1 + p2))
    w2 = g_w * (p2 / (p1 + p2))
    e1 = i1 - N_GROUPS
    e2 = i2 - N_GROUPS

    oh1 = lane == e1
    oh2 = lane == e2
    oh = jnp.where(oh1 | oh2, 1.0, 0.0)
    r_i = lax.broadcasted_iota(jnp.int32, (tm, tm), 0)
    c_i = lax.broadcasted_iota(jnp.int32, (tm, tm), 1)
    ltri = jnp.where(c_i < r_i, 1.0, 0.0).astype(BF16)
    cum = jnp.dot(ltri, oh.astype(BF16), preferred_element_type=F32) + cnt_sc[0:1, :]
    rank1 = jnp.sum(jnp.where(oh1, cum, 0.0), axis=-1, keepdims=True)
    rank2 = jnp.sum(jnp.where(oh2, cum, 0.0), axis=-1, keepdims=True)
    new_cnt = cnt_sc[0:1, :] + jnp.sum(oh, axis=0, keepdims=True)
    cnt_sc[...] = jnp.broadcast_to(new_cnt, cnt_sc.shape)
    cnt_ref[...] = jnp.broadcast_to(new_cnt, cnt_ref.shape)

    ri = jnp.where(lane_i == 0, e1, jnp.where(lane_i == 1, e2,
                   jnp.where(lane_i == 2, rank1, jnp.where(lane_i == 3, rank2, 0.0))))
    ri_ref[...] = ri.astype(jnp.int32)
    rw_ref[...] = jnp.where(lane_i == 0, w1, jnp.where(lane_i == 1, w2, 0.0))


def _post(x2d, ya, yr, ga, gr, woa, wor, gx, wmq, gmq, km, vm, wmo, gmoe, wrt, wrl, brt, seq, tm):
    t, d = x2d.shape
    n_s = seq // tm
    tile = lambda i: (i, 0)
    c2 = lambda i: (0, 0)
    full = lambda a: pl.BlockSpec(a.shape, c2)
    mem_spec = lambda a: pl.BlockSpec((None,) + a.shape[1:], lambda i: (i // n_s, 0, 0))
    kern = functools.partial(_post_kernel, tm=tm)
    return pl.pallas_call(
        kern,
        grid=(t // tm,),
        in_specs=[pl.BlockSpec((tm, d), tile), pl.BlockSpec((tm, ya.shape[1]), tile),
                  pl.BlockSpec((tm, yr.shape[1]), tile), full(ga), full(gr), full(woa), full(wor),
                  full(gx), full(wmq), full(gmq), mem_spec(km), mem_spec(vm), full(wmo),
                  full(gmoe), full(wrt), full(wrl), full(brt)],
        out_specs=[pl.BlockSpec((tm, d), tile), pl.BlockSpec((tm, d), tile),
                   pl.BlockSpec((tm, LANE), tile), pl.BlockSpec((tm, LANE), tile),
                   pl.BlockSpec((8, LANE), c2)],
        out_shape=[jax.ShapeDtypeStruct((t, d), F32), jax.ShapeDtypeStruct((t, d), F32),
                   jax.ShapeDtypeStruct((t, LANE), jnp.int32),
                   jax.ShapeDtypeStruct((t, LANE), F32),
                   jax.ShapeDtypeStruct((8, LANE), F32)],
        scratch_shapes=[pltpu.VMEM((8, LANE), F32)],
        compiler_params=_cparams(("arbitrary",)),
        name="post",
    )(x2d, ya, yr, ga, gr, woa, wor, gx, wmq, gmq, km, vm, wmo, gmoe, wrt, wrl, brt)


def _row_copy(src, dst, sem, src_row, dst_row):
    return pltpu.make_async_copy(src.at[pl.ds(src_row, 1)], dst.at[pl.ds(dst_row, 1)], sem)


def _dispatch_kernel(dest_ref, pend_ref, h_ref, xb_hbm, zero_sc, sem, zsem, *, tm, bm):
    i = pl.program_id(0)

    @pl.when(i == 0)
    def _():
        zero_sc[...] = jnp.zeros(zero_sc.shape, F32)

        def zero_copy(e):
            end = pend_ref[e]
            start = jnp.where(e == 0, 0, pend_ref[jnp.maximum(e - 1, 0)])
            dst = xb_hbm.at[pl.ds(pl.multiple_of(jnp.maximum(end - bm, 0), bm), bm)]
            return end > start, pltpu.make_async_copy(zero_sc, dst, zsem)

        def z_issue(e, c):
            nonempty, cp = zero_copy(e)

            @pl.when(nonempty)
            def _():
                cp.start()
            return c

        def z_drain(e, c):
            nonempty, cp = zero_copy(e)

            @pl.when(nonempty)
            def _():
                cp.wait()
            return c

        def tail_copy(b):
            dst = xb_hbm.at[pl.ds(pl.multiple_of(b * bm, bm), bm)]
            return pltpu.make_async_copy(zero_sc, dst, zsem)

        def t_issue(b, c):
            tail_copy(b).start()
            return c

        def t_drain(b, c):
            tail_copy(b).wait()
            return c

        first_unused = pend_ref[N_EXPERTS - 1] // bm
        n_blk = xb_hbm.shape[0] // bm
        lax.fori_loop(0, N_EXPERTS, z_issue, 0)
        lax.fori_loop(first_unused, n_blk, t_issue, 0)
        lax.fori_loop(0, N_EXPERTS, z_drain, 0)
        lax.fori_loop(first_unused, n_blk, t_drain, 0)

    def issue(r, c):
        tok = i * tm + r
        _row_copy(h_ref, xb_hbm, sem, r, dest_ref[2 * tok]).start()
        _row_copy(h_ref, xb_hbm, sem, r, dest_ref[2 * tok + 1]).start()
        return c

    def drain(r, c):
        _row_copy(h_ref, xb_hbm, sem, r, 0).wait()
        _row_copy(h_ref, xb_hbm, sem, r, 0).wait()
        return c

    lax.fori_loop(0, tm, issue, 0)
    lax.fori_loop(0, tm, drain, 0)


def _dispatch(dest_flat, pad_end, h2, n_rows, tm, bm):
    t, d = h2.shape
    kern = functools.partial(_dispatch_kernel, tm=tm, bm=bm)
    grid_spec = pltpu.PrefetchScalarGridSpec(
        num_scalar_prefetch=2,
        grid=(t // tm,),
        in_specs=[pl.BlockSpec((tm, d), lambda i, de, pe: (i, 0))],
        out_specs=pl.BlockSpec(memory_space=pl.ANY),
        scratch_shapes=[pltpu.VMEM((bm, d), F32), pltpu.SemaphoreType.DMA(()),
                        pltpu.SemaphoreType.DMA(())],
    )
    return pl.pallas_call(
        kern,
        grid_spec=grid_spec,
        out_shape=jax.ShapeDtypeStruct((n_rows, d), F32),
        compiler_params=_cparams(("arbitrary",)),
        name="dispatch",
    )(dest_flat, pad_end, h2)


def _expert_kernel(blk_e_ref, nused_ref, x_ref, wg_ref, wu_ref, wd_ref, y_ref,
                   wg_sc, wu_sc, wd_sc):
    i = pl.program_id(0)
    used = i < nused_ref[0]

    @pl.when(used)
    def _():
        prev = jnp.maximum(i - 1, 0)

        @pl.when((i == 0) | (blk_e_ref[i] != blk_e_ref[prev]))
        def _():
            wg_sc[...] = wg_ref[...].astype(BF16)
            wu_sc[...] = wu_ref[...].astype(BF16)
            wd_sc[...] = wd_ref[...].astype(BF16)

        x = x_ref[...].astype(BF16)
        g = jnp.dot(x, wg_sc[...], preferred_element_type=F32)
        u = jnp.dot(x, wu_sc[...], preferred_element_type=F32)
        act = (g * jax.nn.sigmoid(g) * u).astype(BF16)
        y_ref[...] = jnp.dot(act, wd_sc[...], preferred_element_type=F32)

    @pl.when(jnp.logical_not(used))
    def _():
        y_ref[...] = jnp.zeros(y_ref.shape, F32)


def _experts(blk_e, nused, xb, w_gate, w_up, w_down, layer, bm):
    n_rows, d = xb.shape
    de = w_gate.shape[-1]
    wmap = lambda i, be, nu: (layer, be[i], 0, 0)
    grid_spec = pltpu.PrefetchScalarGridSpec(
        num_scalar_prefetch=2,
        grid=(n_rows // bm,),
        in_specs=[pl.BlockSpec((bm, d), lambda i, be, nu: (jnp.minimum(i, nu[0] - 1), 0)),
                  pl.BlockSpec((None, None, d, de), wmap),
                  pl.BlockSpec((None, None, d, de), wmap),
                  pl.BlockSpec((None, None, de, d), wmap)],
        out_specs=pl.BlockSpec((bm, d), lambda i, be, nu: (i, 0)),
        scratch_shapes=[pltpu.VMEM((d, de), BF16), pltpu.VMEM((d, de), BF16),
                        pltpu.VMEM((de, d), BF16)],
    )
    return pl.pallas_call(
        _expert_kernel,
        grid_spec=grid_spec,
        out_shape=jax.ShapeDtypeStruct((n_rows, d), F32),
        compiler_params=_cparams(("arbitrary",)),
        name="experts",
    )(blk_e, nused, xb, w_gate, w_up, w_down)


def _combine_kernel(dest_ref, x_ref, rw_ref, y_hbm, o_ref, buf0, buf1, sem, *, tm):
    i = pl.program_id(0)

    def issue(r, c):
        tok = i * tm + r
        _row_copy(y_hbm, buf0, sem, dest_ref[2 * tok], r).start()
        _row_copy(y_hbm, buf1, sem, dest_ref[2 * tok + 1], r).start()
        return c

    lax.fori_loop(0, tm, issue, 0)

    def drain(r, c):
        _row_copy(y_hbm, buf0, sem, 0, r).wait()
        _row_copy(y_hbm, buf1, sem, 0, r).wait()
        return c

    lax.fori_loop(0, tm, drain, 0)
    rw = rw_ref[...]
    o_ref[...] = x_ref[...] + rw[:, 0:1] * buf0[...] + rw[:, 1:2] * buf1[...]


def _combine(dest_flat, x2, rw, yb, tm):
    t, d = x2.shape
    kern = functools.partial(_combine_kernel, tm=tm)
    grid_spec = pltpu.PrefetchScalarGridSpec(
        num_scalar_prefetch=1,
        grid=(t // tm,),
        in_specs=[pl.BlockSpec((tm, d), lambda i, de: (i, 0)),
                  pl.BlockSpec((tm, LANE), lambda i, de: (i, 0)),
                  pl.BlockSpec(memory_space=pl.ANY)],
        out_specs=pl.BlockSpec((tm, d), lambda i, de: (i, 0)),
        scratch_shapes=[pltpu.VMEM((tm, d), F32), pltpu.VMEM((tm, d), F32),
                        pltpu.SemaphoreType.DMA(())],
    )
    return pl.pallas_call(
        kern,
        grid_spec=grid_spec,
        out_shape=jax.ShapeDtypeStruct((t, d), F32),
        compiler_params=_cparams(("arbitrary",)),
        name="combine",
    )(dest_flat, x2, rw, yb)


def _slot_gain(g):
    return jnp.concatenate([g, jnp.zeros((HEAD_SLOT - QK_HEAD_DIM,), F32)])[None, :]


def _rope_slot_tables(seq):
    inv = 1.0 / (ROPE_BASE ** (jnp.arange(0, QK_ROPE_DIM, 2, dtype=F32) / QK_ROPE_DIM))
    ang = jnp.arange(seq, dtype=F32)[:, None] * inv[None, :]
    cos, sin = jnp.cos(ang), jnp.sin(ang)
    zero = jnp.zeros_like(cos)
    cos_t = jnp.concatenate([cos, cos, zero, zero], axis=-1)
    sin_a = jnp.concatenate([-sin, zero, zero, zero], axis=-1)
    sin_b = jnp.concatenate([zero, sin, zero, zero], axis=-1)
    return cos_t, sin_a, sin_b


def _pick_tile(n, pref):
    t = min(n, pref)
    while n % t:
        t //= 2
    return t


def kernel(x, mem, mix_norm_g, w_in, q_lora_norm_g, kv_lora_norm_g, w_uq, w_ukv, att_q_norm_g, att_k_norm_g, conv_w, conv_b, w_rgate, b_rgate, w_igate, b_igate, lru_lambda, att_out_norm_g, rnn_out_norm_g, w_out, xattn_norm_g, mem_norm_g, w_mq, w_mk, w_mv, mem_q_norm_g, mem_k_norm_g, w_mo, moe_norm_g, w_router_group, b_router_group, w_router_expert, b_router_expert, w_exp_gate, w_exp_up, w_exp_down):
    batch, seq, d = x.shape
    depth = w_in.shape[0]
    t = batch * seq
    d_att = N_ATT_HEADS * LANE
    d_rnn = conv_w.shape[-1]
    tm = _pick_tile(seq, 256)
    tc = _pick_tile(seq, 512)
    nc = 1
    bm = 256
    n_asg = 2 * t
    n_rows = (-(-n_asg // bm) + N_EXPERTS) * bm
    n_blk = n_rows // bm

    cos_t, sin_a, sin_b = _rope_slot_tables(seq)
    row = lambda a: a[None, :]
    xc = x.reshape(t, d)

    for l in range(depth):
        wl = w_in[l]
        o_pe = Q_LORA_RANK + KV_LORA_RANK
        win = jnp.concatenate([wl[:, :o_pe + QK_ROPE_DIM],
                               jnp.zeros((d, LANE - QK_ROPE_DIM), F32),
                               wl[:, o_pe + QK_ROPE_DIM:]], axis=1).astype(BF16)
        wuq = jnp.pad(w_uq[l].reshape(Q_LORA_RANK, N_ATT_HEADS, QK_HEAD_DIM),
                      ((0, 0), (0, 0), (0, HEAD_SLOT - QK_HEAD_DIM))
                      ).reshape(Q_LORA_RANK, N_ATT_HEADS * HEAD_SLOT).astype(BF16)
        wukv = w_ukv[l].reshape(KV_LORA_RANK, N_ATT_HEADS, 2 * LANE)
        wuk = wukv[:, :, :LANE].reshape(KV_LORA_RANK, d_att).astype(BF16)
        wuv = wukv[:, :, LANE:].reshape(KV_LORA_RANK, d_att).astype(BF16)

        q, k, v, xr, gate = _inproj(
            xc, row(mix_norm_g[l]), win, row(q_lora_norm_g[l]), row(kv_lora_norm_g[l]),
            wuq, wuk, wuv, _slot_gain(att_q_norm_g[l]), _slot_gain(att_k_norm_g[l]),
            cos_t, sin_a, sin_b, seq, tm)

        y_att = _attention(q.reshape(batch, seq, -1), k.reshape(batch, seq, -1),
                           v.reshape(batch, seq, -1), tc, nc).reshape(t, d_att)

        y_rnn = _rglru(xr, gate, conv_w[l], row(conv_b[l]), w_rgate[l].astype(BF16),
                       row(b_rgate[l]), w_igate[l].astype(BF16), row(b_igate[l]),
                       row(lru_lambda[l]), batch, seq, tm)

        km, vm = _memkv(mem, row(mem_norm_g[l]), w_mk[l].astype(BF16), w_mv[l].astype(BF16),
                        row(mem_k_norm_g[l]))

        wrt = jnp.concatenate([w_router_group[l], w_router_expert[l],
                               jnp.zeros((d, LANE - N_GROUPS - N_EXPERTS), F32)], axis=1)
        brt = jnp.concatenate([b_router_group[l], b_router_expert[l],
                               jnp.zeros((LANE - N_GROUPS - N_EXPERTS,), F32)])[None, :]
        wrt_hi = wrt.astype(BF16)
        wrt_lo = (wrt - wrt_hi.astype(F32)).astype(BF16)
        wo = w_out[l].astype(BF16)
        x2, h2, ri, rw, cnt = _post(
            xc, y_att, y_rnn, row(att_out_norm_g[l]), row(rnn_out_norm_g[l]),
            wo[:d_att], wo[d_att:], row(xattn_norm_g[l]), w_mq[l].astype(BF16),
            row(mem_q_norm_g[l]), km, vm, w_mo[l].astype(BF16), row(moe_norm_g[l]),
            wrt_hi, wrt_lo, brt, seq, tm)

        counts = cnt[0, :N_EXPERTS].astype(jnp.int32)
        padded = (counts + bm - 1) // bm * bm
        pad_end = jnp.cumsum(padded)
        pad_start = pad_end - padded
        eid = ri[:, 0:2]
        dest = (pad_start[eid] + ri[:, 2:4]).reshape(n_asg)
        blk_start = jnp.arange(n_blk, dtype=jnp.int32) * bm
        blk_e = jnp.minimum(jnp.sum((pad_end[None, :] <= blk_start[:, None]).astype(jnp.int32), axis=1),
                            N_EXPERTS - 1)
        nused = pad_end[-1:] // bm

        xb = _dispatch(dest, pad_end, h2, n_rows, tm, bm)
        yb = _experts(blk_e, nused, xb, w_exp_gate, w_exp_up, w_exp_down, l, bm)
        xc = _combine(dest, x2, rw, yb, tm)

    return xc.reshape(batch, seq, d)
```

```python
import functools

import jax
import jax.numpy as jnp
from jax import lax
from jax.experimental import pallas as pl
from jax.experimental.pallas import tpu as pltpu

EPS = 1e-6
N_ATT_HEADS = 8
QK_NOPE_DIM = 128
QK_ROPE_DIM = 64
QK_HEAD_DIM = QK_NOPE_DIM + QK_ROPE_DIM
Q_LORA_RANK = 512
KV_LORA_RANK = 256
ROPE_BASE = 10000.0
N_RNN_BLOCKS = 8
CONV_WIDTH = 4
LRU_C = 8.0
N_MEM_HEADS = 4
MEM_HEAD_DIM = 128
N_GROUPS = 4
EXPERTS_PER_GROUP = 8
N_EXPERTS = N_GROUPS * EXPERTS_PER_GROUP

LANE = 128
HEAD_SLOT = 2 * LANE
NEG_BIG = -1e30
LOG2_E = 1.4426950408889634

VMEM_LIMIT = 56 * 1024 * 1024

F32 = jnp.float32
BF16 = jnp.bfloat16


def _cparams(sem):
    return pltpu.CompilerParams(dimension_semantics=sem, vmem_limit_bytes=VMEM_LIMIT)


def _rms(x, g):
    return x * lax.rsqrt(jnp.mean(x * x, axis=-1, keepdims=True) + EPS) * g


def _rope_slot(r, cos_t, sin_a, sin_b):
    return r * cos_t + pltpu.roll(r, 96, 1) * sin_a + pltpu.roll(r, 32, 1) * sin_b


def _inproj_kernel(x_ref, gmix_ref, win_ref, gql_ref, gkvl_ref, wuq_ref, wuk_ref, wuv_ref,
                   gq_ref, gk_ref, cos_ref, sina_ref, sinb_ref,
                   q_ref, k_ref, v_ref, xr_ref, gate_ref):
    d_rnn = xr_ref.shape[-1]
    h = _rms(x_ref[...], gmix_ref[...]).astype(BF16)
    z = jnp.dot(h, win_ref[...], preferred_element_type=F32)
    o_kv = Q_LORA_RANK
    o_pe = o_kv + KV_LORA_RANK
    o_xr = o_pe + LANE
    o_gate = o_xr + d_rnn
    xr_ref[...] = z[:, o_xr:o_gate]
    gate_ref[...] = z[:, o_gate:o_gate + d_rnn]

    cos_t = cos_ref[...]
    sin_a = sina_ref[...]
    sin_b = sinb_ref[...]
    inv_d = 1.0 / QK_HEAD_DIM
    scale = LOG2_E * QK_HEAD_DIM ** -0.5

    cq = _rms(z[:, :o_kv], gql_ref[...]).astype(BF16)
    qf = jnp.dot(cq, wuq_ref[...], preferred_element_type=F32)
    gq = gq_ref[...]
    for hd in range(N_ATT_HEADS):
        slot = qf[:, hd * HEAD_SLOT:(hd + 1) * HEAD_SLOT]
        ss = jnp.sum(slot * slot, axis=-1, keepdims=True) * inv_d
        sn = slot * lax.rsqrt(ss + EPS) * gq
        q_ref[:, hd * HEAD_SLOT:hd * HEAD_SLOT + LANE] = (sn[:, :LANE] * scale).astype(BF16)
        q_ref[:, hd * HEAD_SLOT + LANE:(hd + 1) * HEAD_SLOT] = (
            _rope_slot(sn[:, LANE:], cos_t, sin_a, sin_b) * scale).astype(BF16)

    ckv = _rms(z[:, o_kv:o_pe], gkvl_ref[...]).astype(BF16)
    kn = jnp.dot(ckv, wuk_ref[...], preferred_element_type=F32)
    vv = jnp.dot(ckv, wuv_ref[...], preferred_element_type=F32).astype(BF16)
    ones = jnp.ones((vv.shape[0], LANE), BF16)
    for hd in range(N_ATT_HEADS):
        v_ref[:, hd * HEAD_SLOT:hd * HEAD_SLOT + LANE] = vv[:, hd * LANE:(hd + 1) * LANE]
        v_ref[:, hd * HEAD_SLOT + LANE:(hd + 1) * HEAD_SLOT] = ones
    kpe = z[:, o_pe:o_xr]
    pe_ss = jnp.sum(kpe * kpe, axis=-1, keepdims=True)
    gk = gk_ref[...]
    for hd in range(N_ATT_HEADS):
        kh = kn[:, hd * LANE:(hd + 1) * LANE]
        ss = (jnp.sum(kh * kh, axis=-1, keepdims=True) + pe_ss) * inv_d
        inv = lax.rsqrt(ss + EPS)
        k_ref[:, hd * HEAD_SLOT:hd * HEAD_SLOT + LANE] = (kh * inv * gk[:, :LANE]).astype(BF16)
        k_ref[:, hd * HEAD_SLOT + LANE:(hd + 1) * HEAD_SLOT] = _rope_slot(
            kpe * inv * gk[:, LANE:], cos_t, sin_a, sin_b).astype(BF16)


def _inproj(x2d, gmix, win, gql, gkvl, wuq, wuk, wuv, gq, gk, cos_t, sin_a, sin_b, seq, tm):
    t, d = x2d.shape
    d_rnn = (win.shape[1] - Q_LORA_RANK - KV_LORA_RANK - LANE) // 2
    n_s = seq // tm
    const = lambda i: (0, 0)
    tile = lambda i: (i, 0)
    pos = lambda i: (i % n_s, 0)
    full = lambda a: pl.BlockSpec(a.shape, const)
    return pl.pallas_call(
        _inproj_kernel,
        grid=(t // tm,),
        in_specs=[pl.BlockSpec((tm, d), tile), full(gmix), full(win), full(gql), full(gkvl),
                  full(wuq), full(wuk), full(wuv), full(gq), full(gk),
                  pl.BlockSpec((tm, LANE), pos), pl.BlockSpec((tm, LANE), pos),
                  pl.BlockSpec((tm, LANE), pos)],
        out_specs=[pl.BlockSpec((tm, N_ATT_HEADS * HEAD_SLOT), tile),
                   pl.BlockSpec((tm, N_ATT_HEADS * HEAD_SLOT), tile),
                   pl.BlockSpec((tm, N_ATT_HEADS * HEAD_SLOT), tile),
                   pl.BlockSpec((tm, d_rnn), tile), pl.BlockSpec((tm, d_rnn), tile)],
        out_shape=[jax.ShapeDtypeStruct((t, N_ATT_HEADS * HEAD_SLOT), BF16),
                   jax.ShapeDtypeStruct((t, N_ATT_HEADS * HEAD_SLOT), BF16),
                   jax.ShapeDtypeStruct((t, N_ATT_HEADS * HEAD_SLOT), BF16),
                   jax.ShapeDtypeStruct((t, d_rnn), F32),
                   jax.ShapeDtypeStruct((t, d_rnn), F32)],
        compiler_params=_cparams(("parallel",)),
        name="inproj",
    )(x2d, gmix, win, gql, gkvl, wuq, wuk, wuv, gq, gk, cos_t, sin_a, sin_b)


def _attn_pairs(nq):
    full = [(i, j) for i in range(nq) for j in range(i)]
    diag = [(i, i) for i in range(nq)]
    return full + diag, len(full)


def _attn_kernel(qi_ref, kj_ref, q_ref, k_ref, v_ref, o_ref, acc_all, m_all,
                 s0, s1, p0, p1, a0, a1, *, tc, n_pairs, n_full):
    m_all[...] = jnp.full(m_all.shape, NEG_BIG, F32)
    acc_all[...] = jnp.zeros(acc_all.shape, F32)
    sb, pb, ab = (s0, s1), (p0, p1), (a0, a1)

    def rows(idx):
        return pl.ds(pl.multiple_of(idx * tc, tc), tc)

    def stage_a(n, par):
        sb[par][...] = lax.dot_general(q_ref[rows(qi_ref[n]), :], k_ref[rows(kj_ref[n]), :],
                                       (((1,), (1,)), ((), ())), preferred_element_type=F32)

    def stage_b(n, par, masked):
        r = rows(qi_ref[n])
        s = sb[par][...]
        if masked:
            rpos = lax.broadcasted_iota(jnp.int32, (tc, tc), 0)
            cpos = lax.broadcasted_iota(jnp.int32, (tc, tc), 1)
            s = jnp.where(cpos <= rpos, s, NEG_BIG)
        m_prev = m_all[r, :]
        m_new = jnp.maximum(m_prev, jnp.max(s, axis=-1, keepdims=True))
        pb[par][...] = jnp.exp2(s - jnp.tile(m_new, (1, tc // LANE))).astype(BF16)
        ab[par][...] = jnp.exp2(m_prev - m_new)
        m_all[r, :] = m_new

    def stage_c(n, par):
        r = rows(qi_ref[n])
        pv = jnp.dot(pb[par][...], v_ref[rows(kj_ref[n]), :], preferred_element_type=F32)
        acc_all[r, :] = jnp.tile(ab[par][...], (1, 2)) * acc_all[r, :] + pv

    def sub(n, par, do_a, do_b, do_c, masked):
        if do_a:
            stage_a(n + 2, par)
        if do_b:
            stage_b(n + 1, 1 - par, masked)
        if do_c:
            stage_c(n, par)

    runs = []
    for n in range(-2, n_pairs):
        flags = (n + 2 < n_pairs, 0 <= n + 1 < n_pairs, n >= 0, n + 1 >= n_full)
        if runs and runs[-1][1] == flags:
            runs[-1][2] += 1
        else:
            runs.append([n, flags, 1])
    for n_start, flags, count in runs:
        par0 = n_start % 2
        n_loop = count // 2 if all(flags[:3]) else 0
        if n_loop:
            def body(u, carry, n_start=n_start, flags=flags, par0=par0):
                n = n_start + 2 * u
                sub(n, par0, *flags)
                sub(n + 1, 1 - par0, *flags)
                return carry

            lax.fori_loop(0, n_loop, body, 0)
        for n in range(n_start + 2 * n_loop, n_start + count):
            sub(n, n % 2, *flags)

    for it in range(acc_all.shape[0] // tc):
        r = slice(it * tc, (it + 1) * tc)
        o_ref[r, :] = acc_all[r, :LANE] / acc_all[r, LANE:]


def _attention(q, k, v, tc):
    b, s, _ = q.shape
    nq = s // tc
    pairs, n_full = _attn_pairs(nq)
    qi = jnp.asarray([p[0] for p in pairs], jnp.int32)
    kj = jnp.asarray([p[1] for p in pairs], jnp.int32)
    kern = functools.partial(_attn_kernel, tc=tc, n_pairs=len(pairs), n_full=n_full)
    head_blk = lambda bi, h, qi_r, kj_r: (bi, 0, h)
    grid_spec = pltpu.PrefetchScalarGridSpec(
        num_scalar_prefetch=2,
        grid=(b, N_ATT_HEADS),
        in_specs=[pl.BlockSpec((None, s, HEAD_SLOT), head_blk),
                  pl.BlockSpec((None, s, HEAD_SLOT), head_blk),
                  pl.BlockSpec((None, s, HEAD_SLOT), head_blk)],
        out_specs=pl.BlockSpec((None, s, LANE), head_blk),
        scratch_shapes=[pltpu.VMEM((s, HEAD_SLOT), F32), pltpu.VMEM((s, LANE), F32),
                        pltpu.VMEM((tc, tc), F32), pltpu.VMEM((tc, tc), F32),
                        pltpu.VMEM((tc, tc), BF16), pltpu.VMEM((tc, tc), BF16),
                        pltpu.VMEM((tc, LANE), F32), pltpu.VMEM((tc, LANE), F32)],
    )
    return pl.pallas_call(
        kern,
        grid_spec=grid_spec,
        out_shape=jax.ShapeDtypeStruct((b, s, N_ATT_HEADS * LANE), F32),
        compiler_params=_cparams(("parallel", "parallel")),
        name="attn",
    )(qi, kj, q, k, v)


def _rglru_kernel(xr_ref, gate_ref, cw_ref, cb_ref, wr_ref, br_ref, wi_ref, bi_ref, lam_ref,
                  y_ref, tail_sc, h_sc, *, ts):
    i = pl.program_id(1)

    @pl.when(i == 0)
    def _():
        tail_sc[...] = jnp.zeros(tail_sc.shape, F32)
        h_sc[...] = jnp.zeros(h_sc.shape, F32)

    xr = xr_ref[...]
    c = xr.shape[-1]
    ext = jnp.concatenate([tail_sc[...], xr], axis=0)
    cw = cw_ref[...]
    xc = jnp.broadcast_to(cb_ref[...], xr.shape)
    for j in range(CONV_WIDTH):
        off = 8 - (CONV_WIDTH - 1) + j
        xc = xc + ext[off:off + ts, :] * cw[j:j + 1, :]
    tail_sc[...] = xr[ts - 8:, :]

    blk = c // N_RNN_BLOCKS
    xcb = xc.astype(BF16)
    r_parts = []
    i_parts = []
    for n in range(N_RNN_BLOCKS):
        xb = xcb[:, n * blk:(n + 1) * blk]
        r_parts.append(jnp.dot(xb, wr_ref[n], preferred_element_type=F32))
        i_parts.append(jnp.dot(xb, wi_ref[n], preferred_element_type=F32))
    r = jax.nn.sigmoid(jnp.concatenate(r_parts, axis=-1) + br_ref[...])
    ig = jax.nn.sigmoid(jnp.concatenate(i_parts, axis=-1) + bi_ref[...])

    nlam = -lam_ref[...]
    softplus = jnp.maximum(nlam, 0.0) + jnp.log(1.0 + jnp.exp(-jnp.abs(nlam)))
    log_a = (-LRU_C * r) * softplus
    a = jnp.exp(log_a)
    u = jnp.sqrt(1.0 - jnp.exp(2.0 * log_a)) * (ig * xc)

    row = lax.broadcasted_iota(jnp.int32, (ts, 1), 0)
    sh = 1
    while sh < ts:
        a_sh = pltpu.roll(a, sh, 0)
        u_sh = pltpu.roll(u, sh, 0)
        valid = row >= sh
        u = jnp.where(valid, a * u_sh + u, u)
        a = jnp.where(valid, a * a_sh, a)
        sh *= 2
    hcur = u + a * h_sc[0:1, :]
    h_sc[...] = jnp.broadcast_to(hcur[ts - 1:ts, :], h_sc.shape)
    y_ref[...] = hcur * jax.nn.gelu(gate_ref[...])


def _rglru(xr, gate, cw, cb, wr, br, wi, bi, lam, batch, seq, ts):
    t, c = xr.shape
    n_s = seq // ts
    tile = lambda b, i: (b * n_s + i, 0)
    c2 = lambda b, i: (0, 0)
    c3 = lambda b, i: (0, 0, 0)
    kern = functools.partial(_rglru_kernel, ts=ts)
    return pl.pallas_call(
        kern,
        grid=(batch, n_s),
        in_specs=[pl.BlockSpec((ts, c), tile), pl.BlockSpec((ts, c), tile),
                  pl.BlockSpec(cw.shape, c2), pl.BlockSpec(cb.shape, c2),
                  pl.BlockSpec(wr.shape, c3), pl.BlockSpec(br.shape, c2),
                  pl.BlockSpec(wi.shape, c3), pl.BlockSpec(bi.shape, c2),
                  pl.BlockSpec(lam.shape, c2)],
        out_specs=pl.BlockSpec((ts, c), tile),
        out_shape=jax.ShapeDtypeStruct((t, c), F32),
        scratch_shapes=[pltpu.VMEM((8, c), F32), pltpu.VMEM((8, c), F32)],
        compiler_params=_cparams(("parallel", "arbitrary")),
        name="rglru",
    )(xr, gate, cw, cb, wr, br, wi, bi, lam)


def _memkv_kernel(mem_ref, g_ref, wk_ref, wv_ref, gk_ref, k_ref, v_ref):
    m = _rms(mem_ref[...], g_ref[...]).astype(BF16)
    kf = jnp.dot(m, wk_ref[...], preferred_element_type=F32)
    v_ref[...] = jnp.dot(m, wv_ref[...], preferred_element_type=F32).astype(BF16)
    gk = gk_ref[...]
    for hd in range(N_MEM_HEADS):
        sl = slice(hd * MEM_HEAD_DIM, (hd + 1) * MEM_HEAD_DIM)
        k_ref[:, sl] = _rms(kf[:, sl], gk).astype(BF16)


def _memkv(mem, g, wk, wv, gk):
    b, m_len, d = mem.shape
    dk = wk.shape[1]
    c2 = lambda i: (0, 0)
    return pl.pallas_call(
        _memkv_kernel,
        grid=(b,),
        in_specs=[pl.BlockSpec((None, m_len, d), lambda i: (i, 0, 0)), pl.BlockSpec(g.shape, c2),
                  pl.BlockSpec(wk.shape, c2), pl.BlockSpec(wv.shape, c2), pl.BlockSpec(gk.shape, c2)],
        out_specs=[pl.BlockSpec((None, m_len, dk), lambda i: (i, 0, 0)),
                   pl.BlockSpec((None, m_len, dk), lambda i: (i, 0, 0))],
        out_shape=[jax.ShapeDtypeStruct((b, m_len, dk), BF16),
                   jax.ShapeDtypeStruct((b, m_len, dk), BF16)],
        compiler_params=_cparams(("parallel",)),
        name="memkv",
    )(mem, g, wk, wv, gk)


def _post_kernel(x_ref, ya_ref, yr_ref, ga_ref, gr_ref, woa_ref, wor_ref, gx_ref, wmq_ref, gmq_ref,
                 km_ref, vm_ref, wmo_ref, gmoe_ref, wrt_ref, wrl_ref, brt_ref,
                 x2_ref, h2_ref, ri_ref, rw_ref, cnt_ref, cnt_sc, *, tm):
    step = pl.program_id(0)

    @pl.when(step == 0)
    def _():
        cnt_sc[...] = jnp.zeros(cnt_sc.shape, F32)

    ya = _rms(ya_ref[...], ga_ref[...]).astype(BF16)
    yr = _rms(yr_ref[...], gr_ref[...]).astype(BF16)
    x1 = (x_ref[...] + jnp.dot(ya, woa_ref[...], preferred_element_type=F32)
          + jnp.dot(yr, wor_ref[...], preferred_element_type=F32))

    hq = _rms(x1, gx_ref[...]).astype(BF16)
    qf = jnp.dot(hq, wmq_ref[...], preferred_element_type=F32)
    gmq = gmq_ref[...]
    o_parts = []
    for hd in range(N_MEM_HEADS):
        sl = slice(hd * MEM_HEAD_DIM, (hd + 1) * MEM_HEAD_DIM)
        qh = (_rms(qf[:, sl], gmq) * (MEM_HEAD_DIM ** -0.5)).astype(BF16)
        s = lax.dot_general(qh, km_ref[:, sl], (((1,), (1,)), ((), ())), preferred_element_type=F32)
        p = jnp.exp(s - jnp.max(s, axis=-1, keepdims=True))
        p = p / jnp.sum(p, axis=-1, keepdims=True)
        o_parts.append(jnp.dot(p.astype(BF16), vm_ref[:, sl], preferred_element_type=F32))
    o = jnp.concatenate(o_parts, axis=-1).astype(BF16)
    x2 = x1 + jnp.dot(o, wmo_ref[...], preferred_element_type=F32)
    x2_ref[...] = x2

    h2 = _rms(x2, gmoe_ref[...])
    h2_ref[...] = h2
    h_hi = h2.astype(BF16)
    h_lo = (h2 - h_hi.astype(F32)).astype(BF16)
    w_hi = wrt_ref[...]
    logits = (jnp.dot(h_hi, w_hi, preferred_element_type=F32)
              + jnp.dot(h_lo, w_hi, preferred_element_type=F32)
              + jnp.dot(h_hi, wrl_ref[...], preferred_element_type=F32)) + brt_ref[...]
    lane_i = lax.broadcasted_iota(jnp.int32, (tm, LANE), 1)
    lane = lane_i.astype(F32)
    big = float(LANE)
    is_g = lane_i < N_GROUPS
    gl = jnp.where(is_g, logits, NEG_BIG)
    gmax = jnp.max(gl, axis=-1, keepdims=True)
    gidx = jnp.min(jnp.where(gl == gmax, lane, big), axis=-1, keepdims=True)
    g_w = 1.0 / jnp.sum(jnp.where(is_g, jnp.exp(gl - gmax), 0.0), axis=-1, keepdims=True)
    lo = N_GROUPS + gidx * EXPERTS_PER_GROUP
    in_grp = (lane >= lo) & (lane < lo + EXPERTS_PER_GROUP)
    el = jnp.where(in_grp, logits, NEG_BIG)
    m1 = jnp.max(el, axis=-1, keepdims=True)
    i1 = jnp.min(jnp.where(el == m1, lane, big), axis=-1, keepdims=True)
    el2 = jnp.where(lane == i1, NEG_BIG, el)
    m2 = jnp.max(el2, axis=-1, keepdims=True)
    i2 = jnp.min(jnp.where(el2 == m2, lane, big), axis=-1, keepdims=True)
    zsum = jnp.sum(jnp.where(in_grp, jnp.exp(el - m1), 0.0), axis=-1, keepdims=True)
    p1 = 1.0 / zsum
    p2 = jnp.exp(m2 - m1) / zsum
    w1 = g_w * (p1 / (p1 + p2))
    w2 = g_w * (p2 / (p1 + p2))
    e1 = i1 - N_GROUPS
    e2 = i2 - N_GROUPS

    oh1 = lane == e1
    oh2 = lane == e2
    oh = jnp.where(oh1 | oh2, 1.0, 0.0)
    r_i = lax.broadcasted_iota(jnp.int32, (tm, tm), 0)
    c_i = lax.broadcasted_iota(jnp.int32, (tm, tm), 1)
    ltri = jnp.where(c_i < r_i, 1.0, 0.0).astype(BF16)
    cum = jnp.dot(ltri, oh.astype(BF16), preferred_element_type=F32) + cnt_sc[0:1, :]
    rank1 = jnp.sum(jnp.where(oh1, cum, 0.0), axis=-1, keepdims=True)
    rank2 = jnp.sum(jnp.where(oh2, cum, 0.0), axis=-1, keepdims=True)
    new_cnt = cnt_sc[0:1, :] + jnp.sum(oh, axis=0, keepdims=True)
    cnt_sc[...] = jnp.broadcast_to(new_cnt, cnt_sc.shape)
    cnt_ref[...] = jnp.broadcast_to(new_cnt, cnt_ref.shape)

    ri = jnp.where(lane_i == 0, e1, jnp.where(lane_i == 1, e2,
                   jnp.where(lane_i == 2, rank1, jnp.where(lane_i == 3, rank2, 0.0))))
    ri_ref[...] = ri.astype(jnp.int32)
    rw_ref[...] = jnp.where(lane_i == 0, w1, jnp.where(lane_i == 1, w2, 0.0))


def _post(x2d, ya, yr, ga, gr, woa, wor, gx, wmq, gmq, km, vm, wmo, gmoe, wrt, wrl, brt, seq, tm):
    t, d = x2d.shape
    n_s = seq // tm
    tile = lambda i: (i, 0)
    c2 = lambda i: (0, 0)
    full = lambda a: pl.BlockSpec(a.shape, c2)
    mem_spec = lambda a: pl.BlockSpec((None,) + a.shape[1:], lambda i: (i // n_s, 0, 0))
    kern = functools.partial(_post_kernel, tm=tm)
    return pl.pallas_call(
        kern,
        grid=(t // tm,),
        in_specs=[pl.BlockSpec((tm, d), tile), pl.BlockSpec((tm, ya.shape[1]), tile),
                  pl.BlockSpec((tm, yr.shape[1]), tile), full(ga), full(gr), full(woa), full(wor),
                  full(gx), full(wmq), full(gmq), mem_spec(km), mem_spec(vm), full(wmo),
                  full(gmoe), full(wrt), full(wrl), full(brt)],
        out_specs=[pl.BlockSpec((tm, d), tile), pl.BlockSpec((tm, d), tile),
                   pl.BlockSpec((tm, LANE), tile), pl.BlockSpec((tm, LANE), tile),
                   pl.BlockSpec((8, LANE), c2)],
        out_shape=[jax.ShapeDtypeStruct((t, d), F32), jax.ShapeDtypeStruct((t, d), F32),
                   jax.ShapeDtypeStruct((t, LANE), jnp.int32),
                   jax.ShapeDtypeStruct((t, LANE), F32),
                   jax.ShapeDtypeStruct((8, LANE), F32)],
        scratch_shapes=[pltpu.VMEM((8, LANE), F32)],
        compiler_params=_cparams(("arbitrary",)),
        name="post",
    )(x2d, ya, yr, ga, gr, woa, wor, gx, wmq, gmq, km, vm, wmo, gmoe, wrt, wrl, brt)


def _row_copy(src, dst, sem, src_row, dst_row):
    return pltpu.make_async_copy(src.at[pl.ds(src_row, 1)], dst.at[pl.ds(dst_row, 1)], sem)


def _dispatch_kernel(dest_ref, pend_ref, h_hbm, xb_hbm, zero_sc, sem, zsem, *, tm, bm):
    i = pl.program_id(0)

    @pl.when(i == 0)
    def _():
        zero_sc[...] = jnp.zeros(zero_sc.shape, F32)

        def zero_copy(e):
            end = pend_ref[e]
            start = jnp.where(e == 0, 0, pend_ref[jnp.maximum(e - 1, 0)])
            dst = xb_hbm.at[pl.ds(pl.multiple_of(jnp.maximum(end - bm, 0), bm), bm)]
            return end > start, pltpu.make_async_copy(zero_sc, dst, zsem)

        def z_issue(e, c):
            nonempty, cp = zero_copy(e)

            @pl.when(nonempty)
            def _():
                cp.start()
            return c

        def z_drain(e, c):
            nonempty, cp = zero_copy(e)

            @pl.when(nonempty)
            def _():
                cp.wait()
            return c

        def tail_copy(b):
            dst = xb_hbm.at[pl.ds(pl.multiple_of(b * bm, bm), bm)]
            return pltpu.make_async_copy(zero_sc, dst, zsem)

        def t_issue(b, c):
            tail_copy(b).start()
            return c

        def t_drain(b, c):
            tail_copy(b).wait()
            return c

        first_unused = pend_ref[N_EXPERTS - 1] // bm
        n_blk = xb_hbm.shape[0] // bm
        lax.fori_loop(0, N_EXPERTS, z_issue, 0)
        lax.fori_loop(first_unused, n_blk, t_issue, 0)
        lax.fori_loop(0, N_EXPERTS, z_drain, 0)
        lax.fori_loop(first_unused, n_blk, t_drain, 0)

    def issue(r, c):
        tok = i * tm + r
        _row_copy(h_hbm, xb_hbm, sem, tok, dest_ref[2 * tok]).start()
        _row_copy(h_hbm, xb_hbm, sem, tok, dest_ref[2 * tok + 1]).start()
        return c

    lax.fori_loop(0, tm, issue, 0, unroll=8)

    def wait_tile():
        pltpu.make_async_copy(h_hbm.at[pl.ds(0, 2 * tm)], xb_hbm.at[pl.ds(0, 2 * tm)], sem).wait()

    @pl.when(i > 0)
    def _():
        wait_tile()

    @pl.when(i == pl.num_programs(0) - 1)
    def _():
        wait_tile()


def _dispatch(dest_flat, pad_end, h2, n_rows, tm, bm):
    t, d = h2.shape
    kern = functools.partial(_dispatch_kernel, tm=tm, bm=bm)
    grid_spec = pltpu.PrefetchScalarGridSpec(
        num_scalar_prefetch=2,
        grid=(t // tm,),
        in_specs=[pl.BlockSpec(memory_space=pl.ANY)],
        out_specs=pl.BlockSpec(memory_space=pl.ANY),
        scratch_shapes=[pltpu.VMEM((bm, d), F32), pltpu.SemaphoreType.DMA(()),
                        pltpu.SemaphoreType.DMA(())],
    )
    return pl.pallas_call(
        kern,
        grid_spec=grid_spec,
        out_shape=jax.ShapeDtypeStruct((n_rows, d), F32),
        compiler_params=_cparams(("arbitrary",)),
        name="dispatch",
    )(dest_flat, pad_end, h2)


def _expert_kernel(blk_e_ref, nused_ref, x_ref, wg_ref, wu_ref, wd_ref, y_ref,
                   wg_sc, wu_sc, wd_sc):
    i = pl.program_id(0)
    used = i < nused_ref[0]

    @pl.when(used)
    def _():
        prev = jnp.maximum(i - 1, 0)

        @pl.when((i == 0) | (blk_e_ref[i] != blk_e_ref[prev]))
        def _():
            wg_sc[...] = wg_ref[...].astype(BF16)
            wu_sc[...] = wu_ref[...].astype(BF16)
            wd_sc[...] = wd_ref[...].astype(BF16)

        x = x_ref[...].astype(BF16)
        g = jnp.dot(x, wg_sc[...], preferred_element_type=F32)
        u = jnp.dot(x, wu_sc[...], preferred_element_type=F32)
        act = (g * jax.nn.sigmoid(g) * u).astype(BF16)
        y_ref[...] = jnp.dot(act, wd_sc[...], preferred_element_type=F32)

    @pl.when(jnp.logical_not(used))
    def _():
        y_ref[...] = jnp.zeros(y_ref.shape, F32)


def _experts(blk_e, nused, xb, w_gate, w_up, w_down, layer, bm):
    n_rows, d = xb.shape
    de = w_gate.shape[-1]
    wmap = lambda i, be, nu: (layer, be[i], 0, 0)
    grid_spec = pltpu.PrefetchScalarGridSpec(
        num_scalar_prefetch=2,
        grid=(n_rows // bm,),
        in_specs=[pl.BlockSpec((bm, d), lambda i, be, nu: (jnp.minimum(i, nu[0] - 1), 0)),
                  pl.BlockSpec((None, None, d, de), wmap),
                  pl.BlockSpec((None, None, d, de), wmap),
                  pl.BlockSpec((None, None, de, d), wmap)],
        out_specs=pl.BlockSpec((bm, d), lambda i, be, nu: (i, 0)),
        scratch_shapes=[pltpu.VMEM((d, de), BF16), pltpu.VMEM((d, de), BF16),
                        pltpu.VMEM((de, d), BF16)],
    )
    return pl.pallas_call(
        _expert_kernel,
        grid_spec=grid_spec,
        out_shape=jax.ShapeDtypeStruct((n_rows, d), F32),
        compiler_params=_cparams(("arbitrary",)),
        name="experts",
    )(blk_e, nused, xb, w_gate, w_up, w_down)


def _combine_kernel(dest_ref, x_ref, rw_ref, y_hbm, o_ref, buf, sem, *, tm):
    i = pl.program_id(0)
    slot = i % 2

    def gather(tile, to_slot):
        def issue(r, c):
            tok = tile * tm + r
            _row_copy(y_hbm, buf.at[to_slot, 0], sem.at[to_slot], dest_ref[2 * tok], r).start()
            _row_copy(y_hbm, buf.at[to_slot, 1], sem.at[to_slot], dest_ref[2 * tok + 1], r).start()
            return c

        lax.fori_loop(0, tm, issue, 0, unroll=8)

    @pl.when(i == 0)
    def _():
        gather(0, 0)

    @pl.when(i + 1 < pl.num_programs(0))
    def _():
        gather(i + 1, 1 - slot)

    for k in range(2):
        pltpu.make_async_copy(y_hbm.at[pl.ds(0, tm)], buf.at[slot, k], sem.at[slot]).wait()
    rw = rw_ref[...]
    o_ref[...] = x_ref[...] + rw[:, 0:1] * buf[slot, 0] + rw[:, 1:2] * buf[slot, 1]


def _combine(dest_flat, x2, rw, yb, tm):
    t, d = x2.shape
    kern = functools.partial(_combine_kernel, tm=tm)
    grid_spec = pltpu.PrefetchScalarGridSpec(
        num_scalar_prefetch=1,
        grid=(t // tm,),
        in_specs=[pl.BlockSpec((tm, d), lambda i, de: (i, 0)),
                  pl.BlockSpec((tm, LANE), lambda i, de: (i, 0)),
                  pl.BlockSpec(memory_space=pl.ANY)],
        out_specs=pl.BlockSpec((tm, d), lambda i, de: (i, 0)),
        scratch_shapes=[pltpu.VMEM((2, 2, tm, d), F32), pltpu.SemaphoreType.DMA((2,))],
    )
    return pl.pallas_call(
        kern,
        grid_spec=grid_spec,
        out_shape=jax.ShapeDtypeStruct((t, d), F32),
        compiler_params=_cparams(("arbitrary",)),
        name="combine",
    )(dest_flat, x2, rw, yb)


def _slot_gain(g):
    return jnp.concatenate([g, jnp.zeros((HEAD_SLOT - QK_HEAD_DIM,), F32)])[None, :]


def _rope_slot_tables(seq):
    inv = 1.0 / (ROPE_BASE ** (jnp.arange(0, QK_ROPE_DIM, 2, dtype=F32) / QK_ROPE_DIM))
    ang = jnp.arange(seq, dtype=F32)[:, None] * inv[None, :]
    cos, sin = jnp.cos(ang), jnp.sin(ang)
    zero = jnp.zeros_like(cos)
    cos_t = jnp.concatenate([cos, cos, zero, zero], axis=-1)
    sin_a = jnp.concatenate([-sin, zero, zero, zero], axis=-1)
    sin_b = jnp.concatenate([zero, sin, zero, zero], axis=-1)
    return cos_t, sin_a, sin_b


def _pick_tile(n, pref):
    t = min(n, pref)
    while n % t:
        t //= 2
    return t


def kernel(x, mem, mix_norm_g, w_in, q_lora_norm_g, kv_lora_norm_g, w_uq, w_ukv, att_q_norm_g, att_k_norm_g, conv_w, conv_b, w_rgate, b_rgate, w_igate, b_igate, lru_lambda, att_out_norm_g, rnn_out_norm_g, w_out, xattn_norm_g, mem_norm_g, w_mq, w_mk, w_mv, mem_q_norm_g, mem_k_norm_g, w_mo, moe_norm_g, w_router_group, b_router_group, w_router_expert, b_router_expert, w_exp_gate, w_exp_up, w_exp_down):
    batch, seq, d = x.shape
    depth = w_in.shape[0]
    t = batch * seq
    d_att = N_ATT_HEADS * LANE
    d_rnn = conv_w.shape[-1]
    tm = _pick_tile(seq, 256)
    tc = _pick_tile(seq, 512)
    bm = 256
    n_asg = 2 * t
    n_rows = (-(-n_asg // bm) + N_EXPERTS) * bm
    n_blk = n_rows // bm

    cos_t, sin_a, sin_b = _rope_slot_tables(seq)
    row = lambda a: a[None, :]
    xc = x.reshape(t, d)

    for l in range(depth):
        wl = w_in[l]
        o_pe = Q_LORA_RANK + KV_LORA_RANK
        win = jnp.concatenate([wl[:, :o_pe + QK_ROPE_DIM],
                               jnp.zeros((d, LANE - QK_ROPE_DIM), F32),
                               wl[:, o_pe + QK_ROPE_DIM:]], axis=1).astype(BF16)
        wuq = jnp.pad(w_uq[l].reshape(Q_LORA_RANK, N_ATT_HEADS, QK_HEAD_DIM),
                      ((0, 0), (0, 0), (0, HEAD_SLOT - QK_HEAD_DIM))
                      ).reshape(Q_LORA_RANK, N_ATT_HEADS * HEAD_SLOT).astype(BF16)
        wukv = w_ukv[l].reshape(KV_LORA_RANK, N_ATT_HEADS, 2 * LANE)
        wuk = wukv[:, :, :LANE].reshape(KV_LORA_RANK, d_att).astype(BF16)
        wuv = wukv[:, :, LANE:].reshape(KV_LORA_RANK, d_att).astype(BF16)

        q, k, v, xr, gate = _inproj(
            xc, row(mix_norm_g[l]), win, row(q_lora_norm_g[l]), row(kv_lora_norm_g[l]),
            wuq, wuk, wuv, _slot_gain(att_q_norm_g[l]), _slot_gain(att_k_norm_g[l]),
            cos_t, sin_a, sin_b, seq, tm)

        y_att = _attention(q.reshape(batch, seq, -1), k.reshape(batch, seq, -1),
                           v.reshape(batch, seq, -1), tc).reshape(t, d_att)

        y_rnn = _rglru(xr, gate, conv_w[l], row(conv_b[l]), w_rgate[l].astype(BF16),
                       row(b_rgate[l]), w_igate[l].astype(BF16), row(b_igate[l]),
                       row(lru_lambda[l]), batch, seq, tm)

        km, vm = _memkv(mem, row(mem_norm_g[l]), w_mk[l].astype(BF16), w_mv[l].astype(BF16),
                        row(mem_k_norm_g[l]))

        wrt = jnp.concatenate([w_router_group[l], w_router_expert[l],
                               jnp.zeros((d, LANE - N_GROUPS - N_EXPERTS), F32)], axis=1)
        brt = jnp.concatenate([b_router_group[l], b_router_expert[l],
                               jnp.zeros((LANE - N_GROUPS - N_EXPERTS,), F32)])[None, :]
        wrt_hi = wrt.astype(BF16)
        wrt_lo = (wrt - wrt_hi.astype(F32)).astype(BF16)
        wo = w_out[l].astype(BF16)
        x2, h2, ri, rw, cnt = _post(
            xc, y_att, y_rnn, row(att_out_norm_g[l]), row(rnn_out_norm_g[l]),
            wo[:d_att], wo[d_att:], row(xattn_norm_g[l]), w_mq[l].astype(BF16),
            row(mem_q_norm_g[l]), km, vm, w_mo[l].astype(BF16), row(moe_norm_g[l]),
            wrt_hi, wrt_lo, brt, seq, tm)

        counts = cnt[0, :N_EXPERTS].astype(jnp.int32)
        padded = (counts + bm - 1) // bm * bm
        pad_end = jnp.cumsum(padded)
        pad_start = pad_end - padded
        eid = ri[:, 0:2]
        dest = (pad_start[eid] + ri[:, 2:4]).reshape(n_asg)
        blk_start = jnp.arange(n_blk, dtype=jnp.int32) * bm
        blk_e = jnp.minimum(jnp.sum((pad_end[None, :] <= blk_start[:, None]).astype(jnp.int32), axis=1),
                            N_EXPERTS - 1)
        nused = pad_end[-1:] // bm

        xb = _dispatch(dest, pad_end, h2, n_rows, tm, bm)
        yb = _experts(blk_e, nused, xb, w_exp_gate, w_exp_up, w_exp_down, l, bm)
        xc = _combine(dest, x2, rw, yb, tm)

    return xc.reshape(batch, seq, d)
```

```python
import functools

import jax
import jax.numpy as jnp
from jax import lax
from jax.experimental import pallas as pl
from jax.experimental.pallas import tpu as pltpu

EPS = 1e-6
N_ATT_HEADS = 8
QK_NOPE_DIM = 128
QK_ROPE_DIM = 64
QK_HEAD_DIM = QK_NOPE_DIM + QK_ROPE_DIM
Q_LORA_RANK = 512
KV_LORA_RANK = 256
ROPE_BASE = 10000.0
N_RNN_BLOCKS = 8
CONV_WIDTH = 4
LRU_C = 8.0
N_MEM_HEADS = 4
MEM_HEAD_DIM = 128
N_GROUPS = 4
EXPERTS_PER_GROUP = 8
N_EXPERTS = N_GROUPS * EXPERTS_PER_GROUP

LANE = 128
HEAD_SLOT = 2 * LANE
NEG_BIG = -1e30
LOG2_E = 1.4426950408889634
ATTN_UNROLL = 4

VMEM_LIMIT = 56 * 1024 * 1024

F32 = jnp.float32
BF16 = jnp.bfloat16


def _cparams(sem):
    return pltpu.CompilerParams(dimension_semantics=sem, vmem_limit_bytes=VMEM_LIMIT)


def _rms(x, g):
    return x * lax.rsqrt(jnp.mean(x * x, axis=-1, keepdims=True) + EPS) * g


def _rope_slot(r, cos_t, sin_a, sin_b):
    return r * cos_t + pltpu.roll(r, 96, 1) * sin_a + pltpu.roll(r, 32, 1) * sin_b


def _inproj_kernel(x_ref, gmix_ref, win_ref, gql_ref, gkvl_ref, wuq_ref, wuk_ref, wuv_ref,
                   gq_ref, gk_ref, cos_ref, sina_ref, sinb_ref,
                   q_ref, k_ref, v_ref, xr_ref, gate_ref):
    d_rnn = xr_ref.shape[-1]
    h = _rms(x_ref[...], gmix_ref[...]).astype(BF16)
    z = jnp.dot(h, win_ref[...], preferred_element_type=F32)
    o_kv = Q_LORA_RANK
    o_pe = o_kv + KV_LORA_RANK
    o_xr = o_pe + LANE
    o_gate = o_xr + d_rnn
    xr_ref[...] = z[:, o_xr:o_gate]
    gate_ref[...] = z[:, o_gate:o_gate + d_rnn]

    cos_t = cos_ref[...]
    sin_a = sina_ref[...]
    sin_b = sinb_ref[...]
    inv_d = 1.0 / QK_HEAD_DIM
    scale = LOG2_E * QK_HEAD_DIM ** -0.5

    cq = _rms(z[:, :o_kv], gql_ref[...]).astype(BF16)
    qf = jnp.dot(cq, wuq_ref[...], preferred_element_type=F32)
    gq = gq_ref[...]
    for hd in range(N_ATT_HEADS):
        slot = qf[:, hd * HEAD_SLOT:(hd + 1) * HEAD_SLOT]
        ss = jnp.sum(slot * slot, axis=-1, keepdims=True) * inv_d
        sn = slot * lax.rsqrt(ss + EPS) * gq
        q_ref[:, hd * HEAD_SLOT:hd * HEAD_SLOT + LANE] = (sn[:, :LANE] * scale).astype(BF16)
        q_ref[:, hd * HEAD_SLOT + LANE:(hd + 1) * HEAD_SLOT] = (
            _rope_slot(sn[:, LANE:], cos_t, sin_a, sin_b) * scale).astype(BF16)

    ckv = _rms(z[:, o_kv:o_pe], gkvl_ref[...]).astype(BF16)
    kn = jnp.dot(ckv, wuk_ref[...], preferred_element_type=F32)
    vv = jnp.dot(ckv, wuv_ref[...], preferred_element_type=F32).astype(BF16)
    ones = jnp.ones((vv.shape[0], LANE), BF16)
    for hd in range(N_ATT_HEADS):
        v_ref[:, hd * HEAD_SLOT:hd * HEAD_SLOT + LANE] = vv[:, hd * LANE:(hd + 1) * LANE]
        v_ref[:, hd * HEAD_SLOT + LANE:(hd + 1) * HEAD_SLOT] = ones
    kpe = z[:, o_pe:o_xr]
    pe_ss = jnp.sum(kpe * kpe, axis=-1, keepdims=True)
    gk = gk_ref[...]
    for hd in range(N_ATT_HEADS):
        kh = kn[:, hd * LANE:(hd + 1) * LANE]
        ss = (jnp.sum(kh * kh, axis=-1, keepdims=True) + pe_ss) * inv_d
        inv = lax.rsqrt(ss + EPS)
        k_ref[:, hd * HEAD_SLOT:hd * HEAD_SLOT + LANE] = (kh * inv * gk[:, :LANE]).astype(BF16)
        k_ref[:, hd * HEAD_SLOT + LANE:(hd + 1) * HEAD_SLOT] = _rope_slot(
            kpe * inv * gk[:, LANE:], cos_t, sin_a, sin_b).astype(BF16)


def _inproj(x2d, gmix, win, gql, gkvl, wuq, wuk, wuv, gq, gk, cos_t, sin_a, sin_b, seq, tm):
    t, d = x2d.shape
    d_rnn = (win.shape[1] - Q_LORA_RANK - KV_LORA_RANK - LANE) // 2
    n_s = seq // tm
    const = lambda i: (0, 0)
    tile = lambda i: (i, 0)
    pos = lambda i: (i % n_s, 0)
    full = lambda a: pl.BlockSpec(a.shape, const)
    return pl.pallas_call(
        _inproj_kernel,
        grid=(t // tm,),
        in_specs=[pl.BlockSpec((tm, d), tile), full(gmix), full(win), full(gql), full(gkvl),
                  full(wuq), full(wuk), full(wuv), full(gq), full(gk),
                  pl.BlockSpec((tm, LANE), pos), pl.BlockSpec((tm, LANE), pos),
                  pl.BlockSpec((tm, LANE), pos)],
        out_specs=[pl.BlockSpec((tm, N_ATT_HEADS * HEAD_SLOT), tile),
                   pl.BlockSpec((tm, N_ATT_HEADS * HEAD_SLOT), tile),
                   pl.BlockSpec((tm, N_ATT_HEADS * HEAD_SLOT), tile),
                   pl.BlockSpec((tm, d_rnn), tile), pl.BlockSpec((tm, d_rnn), tile)],
        out_shape=[jax.ShapeDtypeStruct((t, N_ATT_HEADS * HEAD_SLOT), BF16),
                   jax.ShapeDtypeStruct((t, N_ATT_HEADS * HEAD_SLOT), BF16),
                   jax.ShapeDtypeStruct((t, N_ATT_HEADS * HEAD_SLOT), BF16),
                   jax.ShapeDtypeStruct((t, d_rnn), F32),
                   jax.ShapeDtypeStruct((t, d_rnn), F32)],
        compiler_params=_cparams(("parallel",)),
        name="inproj",
    )(x2d, gmix, win, gql, gkvl, wuq, wuk, wuv, gq, gk, cos_t, sin_a, sin_b)


def _attn_pairs(nq):
    full = [(i, j) for i in range(nq) for j in range(i)]
    diag = [(i, i) for i in range(nq)]
    return full + diag, len(full)


def _attn_kernel(qi_ref, kj_ref, q_ref, k_ref, v_ref, o_ref, acc_all, m_all,
                 s0, s1, p0, p1, a0, a1, *, tc, n_pairs, n_full):
    m_all[...] = jnp.full(m_all.shape, NEG_BIG, F32)
    acc_all[...] = jnp.zeros(acc_all.shape, F32)
    sb, pb, ab = (s0, s1), (p0, p1), (a0, a1)

    def rows(idx):
        return pl.ds(pl.multiple_of(idx * tc, tc), tc)

    def stage_a(n, par):
        sb[par][...] = lax.dot_general(q_ref[rows(qi_ref[n]), :], k_ref[rows(kj_ref[n]), :],
                                       (((1,), (1,)), ((), ())), preferred_element_type=F32)

    def stage_b(n, par, masked):
        r = rows(qi_ref[n])
        s = sb[par][...]
        if masked:
            rpos = lax.broadcasted_iota(jnp.int32, (tc, tc), 0)
            cpos = lax.broadcasted_iota(jnp.int32, (tc, tc), 1)
            s = jnp.where(cpos <= rpos, s, NEG_BIG)
        m_prev = m_all[r, :]
        m_new = jnp.maximum(m_prev, jnp.max(s, axis=-1, keepdims=True))
        pb[par][...] = jnp.exp2(s - jnp.tile(m_new, (1, tc // LANE))).astype(BF16)
        ab[par][...] = jnp.exp2(m_prev - m_new)
        m_all[r, :] = m_new

    def stage_c(n, par):
        r = rows(qi_ref[n])
        pv = jnp.dot(pb[par][...], v_ref[rows(kj_ref[n]), :], preferred_element_type=F32)
        acc_all[r, :] = jnp.tile(ab[par][...], (1, 2)) * acc_all[r, :] + pv

    def sub(n, par, do_a, do_b, do_c, masked):
        if do_a:
            stage_a(n + 2, par)
        if do_b:
            stage_b(n + 1, 1 - par, masked)
        if do_c:
            stage_c(n, par)

    runs = []
    for n in range(-2, n_pairs):
        flags = (n + 2 < n_pairs, 0 <= n + 1 < n_pairs, n >= 0, n + 1 >= n_full)
        if runs and runs[-1][1] == flags:
            runs[-1][2] += 1
        else:
            runs.append([n, flags, 1])
    for n_start, flags, count in runs:
        par0 = n_start % 2
        n_loop = count // ATTN_UNROLL if all(flags[:3]) else 0
        if n_loop:
            def body(u, carry, n_start=n_start, flags=flags, par0=par0):
                n = n_start + ATTN_UNROLL * u
                for k in range(ATTN_UNROLL):
                    sub(n + k, (par0 + k) % 2, *flags)
                return carry

            lax.fori_loop(0, n_loop, body, 0)
        for n in range(n_start + ATTN_UNROLL * n_loop, n_start + count):
            sub(n, n % 2, *flags)

    for it in range(acc_all.shape[0] // tc):
        r = slice(it * tc, (it + 1) * tc)
        o_ref[r, :] = acc_all[r, :LANE] / acc_all[r, LANE:]


def _attention(q, k, v, tc):
    b, s, _ = q.shape
    nq = s // tc
    pairs, n_full = _attn_pairs(nq)
    qi = jnp.asarray([p[0] for p in pairs], jnp.int32)
    kj = jnp.asarray([p[1] for p in pairs], jnp.int32)
    kern = functools.partial(_attn_kernel, tc=tc, n_pairs=len(pairs), n_full=n_full)
    head_blk = lambda bi, h, qi_r, kj_r: (bi, 0, h)
    grid_spec = pltpu.PrefetchScalarGridSpec(
        num_scalar_prefetch=2,
        grid=(b, N_ATT_HEADS),
        in_specs=[pl.BlockSpec((None, s, HEAD_SLOT), head_blk),
                  pl.BlockSpec((None, s, HEAD_SLOT), head_blk),
                  pl.BlockSpec((None, s, HEAD_SLOT), head_blk)],
        out_specs=pl.BlockSpec((None, s, LANE), head_blk),
        scratch_shapes=[pltpu.VMEM((s, HEAD_SLOT), F32), pltpu.VMEM((s, LANE), F32),
                        pltpu.VMEM((tc, tc), F32), pltpu.VMEM((tc, tc), F32),
                        pltpu.VMEM((tc, tc), BF16), pltpu.VMEM((tc, tc), BF16),
                        pltpu.VMEM((tc, LANE), F32), pltpu.VMEM((tc, LANE), F32)],
    )
    return pl.pallas_call(
        kern,
        grid_spec=grid_spec,
        out_shape=jax.ShapeDtypeStruct((b, s, N_ATT_HEADS * LANE), F32),
        compiler_params=_cparams(("parallel", "parallel")),
        name="attn",
    )(qi, kj, q, k, v)


def _rglru_kernel(xr_ref, gate_ref, cw_ref, cb_ref, wr_ref, br_ref, wi_ref, bi_ref, lam_ref,
                  y_ref, tail_sc, h_sc, *, ts):
    i = pl.program_id(1)

    @pl.when(i == 0)
    def _():
        tail_sc[...] = jnp.zeros(tail_sc.shape, F32)
        h_sc[...] = jnp.zeros(h_sc.shape, F32)

    xr = xr_ref[...]
    c = xr.shape[-1]
    ext = jnp.concatenate([tail_sc[...], xr], axis=0)
    cw = cw_ref[...]
    xc = jnp.broadcast_to(cb_ref[...], xr.shape)
    for j in range(CONV_WIDTH):
        off = 8 - (CONV_WIDTH - 1) + j
        xc = xc + ext[off:off + ts, :] * cw[j:j + 1, :]
    tail_sc[...] = xr[ts - 8:, :]

    blk = c // N_RNN_BLOCKS
    xcb = xc.astype(BF16)
    r_parts = []
    i_parts = []
    for n in range(N_RNN_BLOCKS):
        xb = xcb[:, n * blk:(n + 1) * blk]
        r_parts.append(jnp.dot(xb, wr_ref[n], preferred_element_type=F32))
        i_parts.append(jnp.dot(xb, wi_ref[n], preferred_element_type=F32))
    r = jax.nn.sigmoid(jnp.concatenate(r_parts, axis=-1) + br_ref[...])
    ig = jax.nn.sigmoid(jnp.concatenate(i_parts, axis=-1) + bi_ref[...])

    nlam = -lam_ref[...]
    softplus = jnp.maximum(nlam, 0.0) + jnp.log(1.0 + jnp.exp(-jnp.abs(nlam)))
    log_a = (-LRU_C * r) * softplus
    a = jnp.exp(log_a)
    u = jnp.sqrt(1.0 - jnp.exp(2.0 * log_a)) * (ig * xc)

    row = lax.broadcasted_iota(jnp.int32, (ts, 1), 0)
    sh = 1
    while sh < ts:
        a_sh = pltpu.roll(a, sh, 0)
        u_sh = pltpu.roll(u, sh, 0)
        valid = row >= sh
        u = jnp.where(valid, a * u_sh + u, u)
        a = jnp.where(valid, a * a_sh, a)
        sh *= 2
    hcur = u + a * h_sc[0:1, :]
    h_sc[...] = jnp.broadcast_to(hcur[ts - 1:ts, :], h_sc.shape)
    y_ref[...] = hcur * jax.nn.gelu(gate_ref[...])


def _rglru(xr, gate, cw, cb, wr, br, wi, bi, lam, batch, seq, ts):
    t, c = xr.shape
    n_s = seq // ts
    tile = lambda b, i: (b * n_s + i, 0)
    c2 = lambda b, i: (0, 0)
    c3 = lambda b, i: (0, 0, 0)
    kern = functools.partial(_rglru_kernel, ts=ts)
    return pl.pallas_call(
        kern,
        grid=(batch, n_s),
        in_specs=[pl.BlockSpec((ts, c), tile), pl.BlockSpec((ts, c), tile),
                  pl.BlockSpec(cw.shape, c2), pl.BlockSpec(cb.shape, c2),
                  pl.BlockSpec(wr.shape, c3), pl.BlockSpec(br.shape, c2),
                  pl.BlockSpec(wi.shape, c3), pl.BlockSpec(bi.shape, c2),
                  pl.BlockSpec(lam.shape, c2)],
        out_specs=pl.BlockSpec((ts, c), tile),
        out_shape=jax.ShapeDtypeStruct((t, c), F32),
        scratch_shapes=[pltpu.VMEM((8, c), F32), pltpu.VMEM((8, c), F32)],
        compiler_params=_cparams(("parallel", "arbitrary")),
        name="rglru",
    )(xr, gate, cw, cb, wr, br, wi, bi, lam)


def _memkv_kernel(mem_ref, g_ref, wk_ref, wv_ref, gk_ref, k_ref, v_ref):
    m = _rms(mem_ref[...], g_ref[...]).astype(BF16)
    kf = jnp.dot(m, wk_ref[...], preferred_element_type=F32)
    v_ref[...] = jnp.dot(m, wv_ref[...], preferred_element_type=F32).astype(BF16)
    gk = gk_ref[...]
    for hd in range(N_MEM_HEADS):
        sl = slice(hd * MEM_HEAD_DIM, (hd + 1) * MEM_HEAD_DIM)
        k_ref[:, sl] = _rms(kf[:, sl], gk).astype(BF16)


def _memkv(mem, g, wk, wv, gk):
    b, m_len, d = mem.shape
    dk = wk.shape[1]
    c2 = lambda i: (0, 0)
    return pl.pallas_call(
        _memkv_kernel,
        grid=(b,),
        in_specs=[pl.BlockSpec((None, m_len, d), lambda i: (i, 0, 0)), pl.BlockSpec(g.shape, c2),
                  pl.BlockSpec(wk.shape, c2), pl.BlockSpec(wv.shape, c2), pl.BlockSpec(gk.shape, c2)],
        out_specs=[pl.BlockSpec((None, m_len, dk), lambda i: (i, 0, 0)),
                   pl.BlockSpec((None, m_len, dk), lambda i: (i, 0, 0))],
        out_shape=[jax.ShapeDtypeStruct((b, m_len, dk), BF16),
                   jax.ShapeDtypeStruct((b, m_len, dk), BF16)],
        compiler_params=_cparams(("parallel",)),
        name="memkv",
    )(mem, g, wk, wv, gk)


def _post_kernel(x_ref, ya_ref, yr_ref, ga_ref, gr_ref, woa_ref, wor_ref, gx_ref, wmq_ref, gmq_ref,
                 km_ref, vm_ref, wmo_ref, gmoe_ref, wrt_ref, wrl_ref, brt_ref,
                 x2_ref, h2_ref, ri_ref, rw_ref, cnt_ref, cnt_sc, *, tm):
    step = pl.program_id(0)

    @pl.when(step == 0)
    def _():
        cnt_sc[...] = jnp.zeros(cnt_sc.shape, F32)

    ya = _rms(ya_ref[...], ga_ref[...]).astype(BF16)
    yr = _rms(yr_ref[...], gr_ref[...]).astype(BF16)
    x1 = (x_ref[...] + jnp.dot(ya, woa_ref[...], preferred_element_type=F32)
          + jnp.dot(yr, wor_ref[...], preferred_element_type=F32))

    hq = _rms(x1, gx_ref[...]).astype(BF16)
    qf = jnp.dot(hq, wmq_ref[...], preferred_element_type=F32)
    gmq = gmq_ref[...]
    o_parts = []
    for hd in range(N_MEM_HEADS):
        sl = slice(hd * MEM_HEAD_DIM, (hd + 1) * MEM_HEAD_DIM)
        qh = (_rms(qf[:, sl], gmq) * (MEM_HEAD_DIM ** -0.5)).astype(BF16)
        s = lax.dot_general(qh, km_ref[:, sl], (((1,), (1,)), ((), ())), preferred_element_type=F32)
        p = jnp.exp(s - jnp.max(s, axis=-1, keepdims=True))
        p = p / jnp.sum(p, axis=-1, keepdims=True)
        o_parts.append(jnp.dot(p.astype(BF16), vm_ref[:, sl], preferred_element_type=F32))
    o = jnp.concatenate(o_parts, axis=-1).astype(BF16)
    x2 = x1 + jnp.dot(o, wmo_ref[...], preferred_element_type=F32)
    x2_ref[...] = x2

    h2 = _rms(x2, gmoe_ref[...])
    h2_ref[...] = h2
    h_hi = h2.astype(BF16)
    h_lo = (h2 - h_hi.astype(F32)).astype(BF16)
    w_hi = wrt_ref[...]
    logits = (jnp.dot(h_hi, w_hi, preferred_element_type=F32)
              + jnp.dot(h_lo, w_hi, preferred_element_type=F32)
              + jnp.dot(h_hi, wrl_ref[...], preferred_element_type=F32)) + brt_ref[...]
    lane_i = lax.broadcasted_iota(jnp.int32, (tm, LANE), 1)
    lane = lane_i.astype(F32)
    big = float(LANE)
    is_g = lane_i < N_GROUPS
    gl = jnp.where(is_g, logits, NEG_BIG)
    gmax = jnp.max(gl, axis=-1, keepdims=True)
    gidx = jnp.min(jnp.where(gl == gmax, lane, big), axis=-1, keepdims=True)
    g_w = 1.0 / jnp.sum(jnp.where(is_g, jnp.exp(gl - gmax), 0.0), axis=-1, keepdims=True)
    lo = N_GROUPS + gidx * EXPERTS_PER_GROUP
    in_grp = (lane >= lo) & (lane < lo + EXPERTS_PER_GROUP)
    el = jnp.where(in_grp, logits, NEG_BIG)
    m1 = jnp.max(el, axis=-1, keepdims=True)
    i1 = jnp.min(jnp.where(el == m1, lane, big), axis=-1, keepdims=True)
    el2 = jnp.where(lane == i1, NEG_BIG, el)
    m2 = jnp.max(el2, axis=-1, keepdims=True)
    i2 = jnp.min(jnp.where(el2 == m2, lane, big), axis=-1, keepdims=True)
    zsum = jnp.sum(jnp.where(in_grp, jnp.exp(el - m1), 0.0), axis=-1, keepdims=True)
    p1 = 1.0 / zsum
    p2 = jnp.exp(m2 - m1) / zsum
    w1 = g_w * (p1 / (p1 + p2))
    w2 = g_w * (p2 / (p1 + p2))
    e1 = i1 - N_GROUPS
    e2 = i2 - N_GROUPS

    oh1 = lane == e1
    oh2 = lane == e2
    oh = jnp.where(oh1 | oh2, 1.0, 0.0)
    r_i = lax.broadcasted_iota(jnp.int32, (tm, tm), 0)
    c_i = lax.broadcasted_iota(jnp.int32, (tm, tm), 1)
    ltri = jnp.where(c_i < r_i, 1.0, 0.0).astype(BF16)
    cum = jnp.dot(ltri, oh.astype(BF16), preferred_element_type=F32) + cnt_sc[0:1, :]
    rank1 = jnp.sum(jnp.where(oh1, cum, 0.0), axis=-1, keepdims=True)
    rank2 = jnp.sum(jnp.where(oh2, cum, 0.0), axis=-1, keepdims=True)
    new_cnt = cnt_sc[0:1, :] + jnp.sum(oh, axis=0, keepdims=True)
    cnt_sc[...] = jnp.broadcast_to(new_cnt, cnt_sc.shape)
    cnt_ref[...] = jnp.broadcast_to(new_cnt, cnt_ref.shape)

    ri = jnp.where(lane_i == 0, e1, jnp.where(lane_i == 1, e2,
                   jnp.where(lane_i == 2, rank1, jnp.where(lane_i == 3, rank2, 0.0))))
    ri_ref[...] = ri.astype(jnp.int32)
    rw_ref[...] = jnp.where(lane_i == 0, w1, jnp.where(lane_i == 1, w2, 0.0))


def _post(x2d, ya, yr, ga, gr, woa, wor, gx, wmq, gmq, km, vm, wmo, gmoe, wrt, wrl, brt, seq, tm):
    t, d = x2d.shape
    n_s = seq // tm
    tile = lambda i: (i, 0)
    c2 = lambda i: (0, 0)
    full = lambda a: pl.BlockSpec(a.shape, c2)
    mem_spec = lambda a: pl.BlockSpec((None,) + a.shape[1:], lambda i: (i // n_s, 0, 0))
    kern = functools.partial(_post_kernel, tm=tm)
    return pl.pallas_call(
        kern,
        grid=(t // tm,),
        in_specs=[pl.BlockSpec((tm, d), tile), pl.BlockSpec((tm, ya.shape[1]), tile),
                  pl.BlockSpec((tm, yr.shape[1]), tile), full(ga), full(gr), full(woa), full(wor),
                  full(gx), full(wmq), full(gmq), mem_spec(km), mem_spec(vm), full(wmo),
                  full(gmoe), full(wrt), full(wrl), full(brt)],
        out_specs=[pl.BlockSpec((tm, d), tile), pl.BlockSpec((tm, d), tile),
                   pl.BlockSpec((tm, LANE), tile), pl.BlockSpec((tm, LANE), tile),
                   pl.BlockSpec((8, LANE), c2)],
        out_shape=[jax.ShapeDtypeStruct((t, d), F32), jax.ShapeDtypeStruct((t, d), F32),
                   jax.ShapeDtypeStruct((t, LANE), jnp.int32),
                   jax.ShapeDtypeStruct((t, LANE), F32),
                   jax.ShapeDtypeStruct((8, LANE), F32)],
        scratch_shapes=[pltpu.VMEM((8, LANE), F32)],
        compiler_params=_cparams(("arbitrary",)),
        name="post",
    )(x2d, ya, yr, ga, gr, woa, wor, gx, wmq, gmq, km, vm, wmo, gmoe, wrt, wrl, brt)


def _row_copy(src, dst, sem, src_row, dst_row):
    return pltpu.make_async_copy(src.at[pl.ds(src_row, 1)], dst.at[pl.ds(dst_row, 1)], sem)


def _dispatch_kernel(dest_ref, pend_ref, h_ref, xb_hbm, zero_sc, sem, zsem, *, tm, bm):
    i = pl.program_id(0)

    @pl.when(i == 0)
    def _():
        zero_sc[...] = jnp.zeros(zero_sc.shape, F32)

        def zero_copy(e):
            end = pend_ref[e]
            start = jnp.where(e == 0, 0, pend_ref[jnp.maximum(e - 1, 0)])
            dst = xb_hbm.at[pl.ds(pl.multiple_of(jnp.maximum(end - bm, 0), bm), bm)]
            return end > start, pltpu.make_async_copy(zero_sc, dst, zsem)

        def z_issue(e, c):
            nonempty, cp = zero_copy(e)

            @pl.when(nonempty)
            def _():
                cp.start()
            return c

        def z_drain(e, c):
            nonempty, cp = zero_copy(e)

            @pl.when(nonempty)
            def _():
                cp.wait()
            return c

        def tail_copy(b):
            dst = xb_hbm.at[pl.ds(pl.multiple_of(b * bm, bm), bm)]
            return pltpu.make_async_copy(zero_sc, dst, zsem)

        def t_issue(b, c):
            tail_copy(b).start()
            return c

        def t_drain(b, c):
            tail_copy(b).wait()
            return c

        first_unused = pend_ref[N_EXPERTS - 1] // bm
        n_blk = xb_hbm.shape[0] // bm
        lax.fori_loop(0, N_EXPERTS, z_issue, 0)
        lax.fori_loop(first_unused, n_blk, t_issue, 0)
        lax.fori_loop(0, N_EXPERTS, z_drain, 0)
        lax.fori_loop(first_unused, n_blk, t_drain, 0)

    base = 2 * tm * i
    for r in range(tm):
        _row_copy(h_ref, xb_hbm, sem, r, dest_ref[base + 2 * r]).start()
        _row_copy(h_ref, xb_hbm, sem, r, dest_ref[base + 2 * r + 1]).start()
    for _ in range(2):
        pltpu.make_async_copy(h_ref, xb_hbm.at[pl.ds(0, tm)], sem).wait()


def _dispatch(dest_flat, pad_end, h2, n_rows, tm, bm):
    t, d = h2.shape
    kern = functools.partial(_dispatch_kernel, tm=tm, bm=bm)
    grid_spec = pltpu.PrefetchScalarGridSpec(
        num_scalar_prefetch=2,
        grid=(t // tm,),
        in_specs=[pl.BlockSpec((tm, d), lambda i, de, pe: (i, 0))],
        out_specs=pl.BlockSpec(memory_space=pl.ANY),
        scratch_shapes=[pltpu.VMEM((bm, d), F32), pltpu.SemaphoreType.DMA(()),
                        pltpu.SemaphoreType.DMA(())],
    )
    return pl.pallas_call(
        kern,
        grid_spec=grid_spec,
        out_shape=jax.ShapeDtypeStruct((n_rows, d), F32),
        compiler_params=_cparams(("arbitrary",)),
        name="dispatch",
    )(dest_flat, pad_end, h2)


def _expert_kernel(blk_e_ref, nused_ref, x_ref, wg_ref, wu_ref, wd_ref, y_ref,
                   wg_sc, wu_sc, wd_sc):
    i = pl.program_id(0)
    used = i < nused_ref[0]

    @pl.when(used)
    def _():
        prev = jnp.maximum(i - 1, 0)

        @pl.when((i == 0) | (blk_e_ref[i] != blk_e_ref[prev]))
        def _():
            wg_sc[...] = wg_ref[...].astype(BF16)
            wu_sc[...] = wu_ref[...].astype(BF16)
            wd_sc[...] = wd_ref[...].astype(BF16)

        x = x_ref[...].astype(BF16)
        g = jnp.dot(x, wg_sc[...], preferred_element_type=F32)
        u = jnp.dot(x, wu_sc[...], preferred_element_type=F32)
        act = (g * jax.nn.sigmoid(g) * u).astype(BF16)
        y_ref[...] = jnp.dot(act, wd_sc[...], preferred_element_type=F32)

    @pl.when(jnp.logical_not(used))
    def _():
        y_ref[...] = jnp.zeros(y_ref.shape, F32)


def _experts(blk_e, nused, xb, w_gate, w_up, w_down, layer, bm):
    n_rows, d = xb.shape
    de = w_gate.shape[-1]
    wmap = lambda i, be, nu: (layer, be[i], 0, 0)
    grid_spec = pltpu.PrefetchScalarGridSpec(
        num_scalar_prefetch=2,
        grid=(n_rows // bm,),
        in_specs=[pl.BlockSpec((bm, d), lambda i, be, nu: (jnp.minimum(i, nu[0] - 1), 0)),
                  pl.BlockSpec((None, None, d, de), wmap),
                  pl.BlockSpec((None, None, d, de), wmap),
                  pl.BlockSpec((None, None, de, d), wmap)],
        out_specs=pl.BlockSpec((bm, d), lambda i, be, nu: (i, 0)),
        scratch_shapes=[pltpu.VMEM((d, de), BF16), pltpu.VMEM((d, de), BF16),
                        pltpu.VMEM((de, d), BF16)],
    )
    return pl.pallas_call(
        _expert_kernel,
        grid_spec=grid_spec,
        out_shape=jax.ShapeDtypeStruct((n_rows, d), F32),
        compiler_params=_cparams(("arbitrary",)),
        name="experts",
    )(blk_e, nused, xb, w_gate, w_up, w_down)


def _combine_kernel(dest_ref, x_ref, rw_ref, y_hbm, o_ref, buf, sem, *, tm):
    i = pl.program_id(0)
    slot = i % 2

    def gather(tile, to_slot):
        base = 2 * tm * tile
        for r in range(tm):
            for k in range(2):
                _row_copy(y_hbm, buf.at[to_slot, k], sem.at[to_slot], dest_ref[base + 2 * r + k], r).start()

    @pl.when(i == 0)
    def _():
        gather(0, 0)

    @pl.when(i + 1 < pl.num_programs(0))
    def _():
        gather(i + 1, 1 - slot)

    for k in range(2):
        pltpu.make_async_copy(y_hbm.at[pl.ds(0, tm)], buf.at[slot, k], sem.at[slot]).wait()
    rw = rw_ref[...]
    o_ref[...] = x_ref[...] + rw[:, 0:1] * buf[slot, 0] + rw[:, 1:2] * buf[slot, 1]


def _combine(dest_flat, x2, rw, yb, tm):
    t, d = x2.shape
    kern = functools.partial(_combine_kernel, tm=tm)
    grid_spec = pltpu.PrefetchScalarGridSpec(
        num_scalar_prefetch=1,
        grid=(t // tm,),
        in_specs=[pl.BlockSpec((tm, d), lambda i, de: (i, 0)),
                  pl.BlockSpec((tm, LANE), lambda i, de: (i, 0)),
                  pl.BlockSpec(memory_space=pl.ANY)],
        out_specs=pl.BlockSpec((tm, d), lambda i, de: (i, 0)),
        scratch_shapes=[pltpu.VMEM((2, 2, tm, d), F32), pltpu.SemaphoreType.DMA((2,))],
    )
    return pl.pallas_call(
        kern,
        grid_spec=grid_spec,
        out_shape=jax.ShapeDtypeStruct((t, d), F32),
        compiler_params=_cparams(("arbitrary",)),
        name="combine",
    )(dest_flat, x2, rw, yb)


def _slot_gain(g):
    return jnp.concatenate([g, jnp.zeros((HEAD_SLOT - QK_HEAD_DIM,), F32)])[None, :]


def _rope_slot_tables(seq):
    inv = 1.0 / (ROPE_BASE ** (jnp.arange(0, QK_ROPE_DIM, 2, dtype=F32) / QK_ROPE_DIM))
    ang = jnp.arange(seq, dtype=F32)[:, None] * inv[None, :]
    cos, sin = jnp.cos(ang), jnp.sin(ang)
    zero = jnp.zeros_like(cos)
    cos_t = jnp.concatenate([cos, cos, zero, zero], axis=-1)
    sin_a = jnp.concatenate([-sin, zero, zero, zero], axis=-1)
    sin_b = jnp.concatenate([zero, sin, zero, zero], axis=-1)
    return cos_t, sin_a, sin_b


def _pick_tile(n, pref):
    t = min(n, pref)
    while n % t:
        t //= 2
    return t


def kernel(x, mem, mix_norm_g, w_in, q_lora_norm_g, kv_lora_norm_g, w_uq, w_ukv, att_q_norm_g, att_k_norm_g, conv_w, conv_b, w_rgate, b_rgate, w_igate, b_igate, lru_lambda, att_out_norm_g, rnn_out_norm_g, w_out, xattn_norm_g, mem_norm_g, w_mq, w_mk, w_mv, mem_q_norm_g, mem_k_norm_g, w_mo, moe_norm_g, w_router_group, b_router_group, w_router_expert, b_router_expert, w_exp_gate, w_exp_up, w_exp_down):
    batch, seq, d = x.shape
    depth = w_in.shape[0]
    t = batch * seq
    d_att = N_ATT_HEADS * LANE
    d_rnn = conv_w.shape[-1]
    tm = _pick_tile(seq, 256)
    tc = _pick_tile(seq, 512)
    bm = 256
    n_asg = 2 * t
    n_rows = (-(-n_asg // bm) + N_EXPERTS) * bm
    n_blk = n_rows // bm

    cos_t, sin_a, sin_b = _rope_slot_tables(seq)
    row = lambda a: a[None, :]
    xc = x.reshape(t, d)

    for l in range(depth):
        wl = w_in[l]
        o_pe = Q_LORA_RANK + KV_LORA_RANK
        win = jnp.concatenate([wl[:, :o_pe + QK_ROPE_DIM],
                               jnp.zeros((d, LANE - QK_ROPE_DIM), F32),
                               wl[:, o_pe + QK_ROPE_DIM:]], axis=1).astype(BF16)
        wuq = jnp.pad(w_uq[l].reshape(Q_LORA_RANK, N_ATT_HEADS, QK_HEAD_DIM),
                      ((0, 0), (0, 0), (0, HEAD_SLOT - QK_HEAD_DIM))
                      ).reshape(Q_LORA_RANK, N_ATT_HEADS * HEAD_SLOT).astype(BF16)
        wukv = w_ukv[l].reshape(KV_LORA_RANK, N_ATT_HEADS, 2 * LANE)
        wuk = wukv[:, :, :LANE].reshape(KV_LORA_RANK, d_att).astype(BF16)
        wuv = wukv[:, :, LANE:].reshape(KV_LORA_RANK, d_att).astype(BF16)

        q, k, v, xr, gate = _inproj(
            xc, row(mix_norm_g[l]), win, row(q_lora_norm_g[l]), row(kv_lora_norm_g[l]),
            wuq, wuk, wuv, _slot_gain(att_q_norm_g[l]), _slot_gain(att_k_norm_g[l]),
            cos_t, sin_a, sin_b, seq, tm)

        y_att = _attention(q.reshape(batch, seq, -1), k.reshape(batch, seq, -1),
                           v.reshape(batch, seq, -1), tc).reshape(t, d_att)

        y_rnn = _rglru(xr, gate, conv_w[l], row(conv_b[l]), w_rgate[l].astype(BF16),
                       row(b_rgate[l]), w_igate[l].astype(BF16), row(b_igate[l]),
                       row(lru_lambda[l]), batch, seq, tm)

        km, vm = _memkv(mem, row(mem_norm_g[l]), w_mk[l].astype(BF16), w_mv[l].astype(BF16),
                        row(mem_k_norm_g[l]))

        wrt = jnp.concatenate([w_router_group[l], w_router_expert[l],
                               jnp.zeros((d, LANE - N_GROUPS - N_EXPERTS), F32)], axis=1)
        brt = jnp.concatenate([b_router_group[l], b_router_expert[l],
                               jnp.zeros((LANE - N_GROUPS - N_EXPERTS,), F32)])[None, :]
        wrt_hi = wrt.astype(BF16)
        wrt_lo = (wrt - wrt_hi.astype(F32)).astype(BF16)
        wo = w_out[l].astype(BF16)
        x2, h2, ri, rw, cnt = _post(
            xc, y_att, y_rnn, row(att_out_norm_g[l]), row(rnn_out_norm_g[l]),
            wo[:d_att], wo[d_att:], row(xattn_norm_g[l]), w_mq[l].astype(BF16),
            row(mem_q_norm_g[l]), km, vm, w_mo[l].astype(BF16), row(moe_norm_g[l]),
            wrt_hi, wrt_lo, brt, seq, tm)

        counts = cnt[0, :N_EXPERTS].astype(jnp.int32)
        padded = (counts + bm - 1) // bm * bm
        pad_end = jnp.cumsum(padded)
        pad_start = pad_end - padded
        eid = ri[:, 0:2]
        dest = (pad_start[eid] + ri[:, 2:4]).reshape(n_asg)
        blk_start = jnp.arange(n_blk, dtype=jnp.int32) * bm
        blk_e = jnp.minimum(jnp.sum((pad_end[None, :] <= blk_start[:, None]).astype(jnp.int32), axis=1),
                            N_EXPERTS - 1)
        nused = pad_end[-1:] // bm

        xb = _dispatch(dest, pad_end, h2, n_rows, tm, bm)
        yb = _experts(blk_e, nused, xb, w_exp_gate, w_exp_up, w_exp_down, l, bm)
        xc = _combine(dest, x2, rw, yb, tm)

    return xc.reshape(batch, seq, d)
```

```python
import functools

import jax
import jax.numpy as jnp
from jax import lax
from jax.experimental import pallas as pl
from jax.experimental.pallas import tpu as pltpu

EPS = 1e-6
N_ATT_HEADS = 8
QK_NOPE_DIM = 128
QK_ROPE_DIM = 64
QK_HEAD_DIM = QK_NOPE_DIM + QK_ROPE_DIM
Q_LORA_RANK = 512
KV_LORA_RANK = 256
ROPE_BASE = 10000.0
N_RNN_BLOCKS = 8
CONV_WIDTH = 4
LRU_C = 8.0
N_MEM_HEADS = 4
MEM_HEAD_DIM = 128
N_GROUPS = 4
EXPERTS_PER_GROUP = 8
N_EXPERTS = N_GROUPS * EXPERTS_PER_GROUP

LANE = 128
HEAD_SLOT = 2 * LANE
NEG_BIG = -1e30
LOG2_E = 1.4426950408889634
ATTN_UNROLL = 4

VMEM_LIMIT = 56 * 1024 * 1024

F32 = jnp.float32
BF16 = jnp.bfloat16


def _cparams(sem):
    return pltpu.CompilerParams(dimension_semantics=sem, vmem_limit_bytes=VMEM_LIMIT)


def _rms(x, g):
    return x * lax.rsqrt(jnp.mean(x * x, axis=-1, keepdims=True) + EPS) * g


def _rope_slot(r, cos_t, sin_a, sin_b):
    return r * cos_t + pltpu.roll(r, 96, 1) * sin_a + pltpu.roll(r, 32, 1) * sin_b


def _lspec(a, layer):
    return pl.BlockSpec((None,) + a.shape[1:], lambda *_: (layer,) + (0,) * (a.ndim - 1))


def _qkv_kernel(x_ref, gmix_ref, win_ref, gql_ref, gkvl_ref, wuq_ref, wuk_ref, wuv_ref,
                gq_ref, gk_ref, cos_ref, sina_ref, sinb_ref, q_ref, k_ref, v_ref):
    h = _rms(x_ref[...], gmix_ref[...]).astype(BF16)
    z = jnp.dot(h, win_ref[...], preferred_element_type=F32)
    o_kv = Q_LORA_RANK
    o_pe = o_kv + KV_LORA_RANK
    o_xr = o_pe + LANE

    cos_t = cos_ref[...]
    sin_a = sina_ref[...]
    sin_b = sinb_ref[...]
    inv_d = 1.0 / QK_HEAD_DIM
    scale = LOG2_E * QK_HEAD_DIM ** -0.5

    cq = _rms(z[:, :o_kv], gql_ref[...]).astype(BF16)
    qf = jnp.dot(cq, wuq_ref[...], preferred_element_type=F32)
    gq = gq_ref[...]
    for hd in range(N_ATT_HEADS):
        slot = qf[:, hd * HEAD_SLOT:(hd + 1) * HEAD_SLOT]
        ss = jnp.sum(slot * slot, axis=-1, keepdims=True) * inv_d
        sn = slot * lax.rsqrt(ss + EPS) * gq
        q_ref[:, hd * HEAD_SLOT:hd * HEAD_SLOT + LANE] = (sn[:, :LANE] * scale).astype(BF16)
        q_ref[:, hd * HEAD_SLOT + LANE:(hd + 1) * HEAD_SLOT] = (
            _rope_slot(sn[:, LANE:], cos_t, sin_a, sin_b) * scale).astype(BF16)

    ckv = _rms(z[:, o_kv:o_pe], gkvl_ref[...]).astype(BF16)
    kn = jnp.dot(ckv, wuk_ref[...], preferred_element_type=F32)
    vv = jnp.dot(ckv, wuv_ref[...], preferred_element_type=F32).astype(BF16)
    ones = jnp.ones((vv.shape[0], LANE), BF16)
    for hd in range(N_ATT_HEADS):
        v_ref[:, hd * HEAD_SLOT:hd * HEAD_SLOT + LANE] = vv[:, hd * LANE:(hd + 1) * LANE]
        v_ref[:, hd * HEAD_SLOT + LANE:(hd + 1) * HEAD_SLOT] = ones
    kpe = z[:, o_pe:o_xr]
    pe_ss = jnp.sum(kpe * kpe, axis=-1, keepdims=True)
    gk = gk_ref[...]
    for hd in range(N_ATT_HEADS):
        kh = kn[:, hd * LANE:(hd + 1) * LANE]
        ss = (jnp.sum(kh * kh, axis=-1, keepdims=True) + pe_ss) * inv_d
        inv = lax.rsqrt(ss + EPS)
        k_ref[:, hd * HEAD_SLOT:hd * HEAD_SLOT + LANE] = (kh * inv * gk[:, :LANE]).astype(BF16)
        k_ref[:, hd * HEAD_SLOT + LANE:(hd + 1) * HEAD_SLOT] = _rope_slot(
            kpe * inv * gk[:, LANE:], cos_t, sin_a, sin_b).astype(BF16)


def _qkv(x2d, params, layer, cos_t, sin_a, sin_b, seq, tm):
    t, d = x2d.shape
    n_s = seq // tm
    tile = lambda i: (i, 0)
    pos = lambda i: (i % n_s, 0)
    slots = N_ATT_HEADS * HEAD_SLOT
    return pl.pallas_call(
        _qkv_kernel,
        grid=(t // tm,),
        in_specs=[pl.BlockSpec((tm, d), tile)] + [_lspec(a, layer) for a in params]
                 + [pl.BlockSpec((tm, LANE), pos)] * 3,
        out_specs=[pl.BlockSpec((tm, slots), tile)] * 3,
        out_shape=[jax.ShapeDtypeStruct((t, slots), BF16)] * 3,
        compiler_params=_cparams(("parallel",)),
        name="qkv",
    )(x2d, *params, cos_t, sin_a, sin_b)


def _attn_pairs(nq):
    full = [(i, j) for i in range(nq) for j in range(i)]
    diag = [(i, i) for i in range(nq)]
    return full + diag, len(full)


def _attn_kernel(qi_ref, kj_ref, q_ref, k_ref, v_ref, o_ref, acc_all, m_all,
                 s0, s1, p0, p1, a0, a1, *, tc, n_pairs, n_full):
    m_all[...] = jnp.full(m_all.shape, NEG_BIG, F32)
    acc_all[...] = jnp.zeros(acc_all.shape, F32)
    sb, pb, ab = (s0, s1), (p0, p1), (a0, a1)

    def rows(idx):
        return pl.ds(pl.multiple_of(idx * tc, tc), tc)

    def stage_a(n, par):
        sb[par][...] = lax.dot_general(q_ref[rows(qi_ref[n]), :], k_ref[rows(kj_ref[n]), :],
                                       (((1,), (1,)), ((), ())), preferred_element_type=F32)

    def stage_b(n, par, masked):
        r = rows(qi_ref[n])
        s = sb[par][...]
        if masked:
            rpos = lax.broadcasted_iota(jnp.int32, (tc, tc), 0)
            cpos = lax.broadcasted_iota(jnp.int32, (tc, tc), 1)
            s = jnp.where(cpos <= rpos, s, NEG_BIG)
        m_prev = m_all[r, :]
        m_new = jnp.maximum(m_prev, jnp.max(s, axis=-1, keepdims=True))
        pb[par][...] = jnp.exp2(s - jnp.tile(m_new, (1, tc // LANE))).astype(BF16)
        ab[par][...] = jnp.exp2(m_prev - m_new)
        m_all[r, :] = m_new

    def stage_c(n, par):
        r = rows(qi_ref[n])
        pv = jnp.dot(pb[par][...], v_ref[rows(kj_ref[n]), :], preferred_element_type=F32)
        acc_all[r, :] = jnp.tile(ab[par][...], (1, 2)) * acc_all[r, :] + pv

    def sub(n, par, do_a, do_b, do_c, masked):
        if do_a:
            stage_a(n + 2, par)
        if do_b:
            stage_b(n + 1, 1 - par, masked)
        if do_c:
            stage_c(n, par)

    runs = []
    for n in range(-2, n_pairs):
        flags = (n + 2 < n_pairs, 0 <= n + 1 < n_pairs, n >= 0, n + 1 >= n_full)
        if runs and runs[-1][1] == flags:
            runs[-1][2] += 1
        else:
            runs.append([n, flags, 1])
    for n_start, flags, count in runs:
        par0 = n_start % 2
        n_loop = count // ATTN_UNROLL if all(flags[:3]) else 0
        if n_loop:
            def body(u, carry, n_start=n_start, flags=flags, par0=par0):
                n = n_start + ATTN_UNROLL * u
                for k in range(ATTN_UNROLL):
                    sub(n + k, (par0 + k) % 2, *flags)
                return carry

            lax.fori_loop(0, n_loop, body, 0)
        for n in range(n_start + ATTN_UNROLL * n_loop, n_start + count):
            sub(n, n % 2, *flags)

    for it in range(acc_all.shape[0] // tc):
        r = slice(it * tc, (it + 1) * tc)
        o_ref[r, :] = acc_all[r, :LANE] / acc_all[r, LANE:]


def _attention(q, k, v, tc):
    b, s, _ = q.shape
    nq = s // tc
    pairs, n_full = _attn_pairs(nq)
    qi = jnp.asarray([p[0] for p in pairs], jnp.int32)
    kj = jnp.asarray([p[1] for p in pairs], jnp.int32)
    kern = functools.partial(_attn_kernel, tc=tc, n_pairs=len(pairs), n_full=n_full)
    head_blk = lambda bi, h, qi_r, kj_r: (bi, 0, h)
    grid_spec = pltpu.PrefetchScalarGridSpec(
        num_scalar_prefetch=2,
        grid=(b, N_ATT_HEADS),
        in_specs=[pl.BlockSpec((None, s, HEAD_SLOT), head_blk),
                  pl.BlockSpec((None, s, HEAD_SLOT), head_blk),
                  pl.BlockSpec((None, s, HEAD_SLOT), head_blk)],
        out_specs=pl.BlockSpec((None, s, LANE), head_blk),
        scratch_shapes=[pltpu.VMEM((s, HEAD_SLOT), F32), pltpu.VMEM((s, LANE), F32),
                        pltpu.VMEM((tc, tc), F32), pltpu.VMEM((tc, tc), F32),
                        pltpu.VMEM((tc, tc), BF16), pltpu.VMEM((tc, tc), BF16),
                        pltpu.VMEM((tc, LANE), F32), pltpu.VMEM((tc, LANE), F32)],
    )
    return pl.pallas_call(
        kern,
        grid_spec=grid_spec,
        out_shape=jax.ShapeDtypeStruct((b, s, N_ATT_HEADS * LANE), F32),
        compiler_params=_cparams(("parallel", "parallel")),
        name="attn",
    )(qi, kj, q, k, v)


def _rglru_kernel(x_ref, gmix_ref, win_ref, cw_ref, cb_ref, wr_ref, br_ref, wi_ref, bi_ref, lam_ref,
                  y_ref, tail_sc, h_sc, *, ts):
    i = pl.program_id(1)

    @pl.when(i == 0)
    def _():
        tail_sc[...] = jnp.zeros(tail_sc.shape, F32)
        h_sc[...] = jnp.zeros(h_sc.shape, F32)

    c = y_ref.shape[-1]
    hn = _rms(x_ref[...], gmix_ref[...]).astype(BF16)
    z = jnp.dot(hn, win_ref[...], preferred_element_type=F32)
    xr = z[:, :c]
    gate = z[:, c:]

    cw = cw_ref[...]
    tail = tail_sc[...]
    row8 = lax.broadcasted_iota(jnp.int32, (8, 1), 0)
    xc = cb_ref[...] + xr * cw[CONV_WIDTH - 1:CONV_WIDTH, :]
    for sh in range(1, CONV_WIDTH):
        rolled = pltpu.roll(xr, sh, 0)
        head = jnp.where(row8 < sh, pltpu.roll(tail, sh, 0), rolled[:8, :])
        shifted = jnp.concatenate([head, rolled[8:, :]], axis=0)
        xc = xc + shifted * cw[CONV_WIDTH - 1 - sh:CONV_WIDTH - sh, :]
    tail_sc[...] = xr[ts - 8:, :]

    blk = c // N_RNN_BLOCKS
    xcb = xc.astype(BF16)
    r_parts = []
    i_parts = []
    for n in range(N_RNN_BLOCKS):
        xb = xcb[:, n * blk:(n + 1) * blk]
        r_parts.append(jnp.dot(xb, wr_ref[n], preferred_element_type=F32))
        i_parts.append(jnp.dot(xb, wi_ref[n], preferred_element_type=F32))
    r = 0.5 * jnp.tanh(0.5 * (jnp.concatenate(r_parts, axis=-1) + br_ref[...])) + 0.5
    ig = 0.5 * jnp.tanh(0.5 * (jnp.concatenate(i_parts, axis=-1) + bi_ref[...])) + 0.5

    nlam = -lam_ref[...]
    softplus = jnp.maximum(nlam, 0.0) + jnp.log(1.0 + jnp.exp(-jnp.abs(nlam)))
    log_a = (-LRU_C * r) * softplus
    a = jnp.exp(log_a)
    u = jnp.sqrt(1.0 - a * a) * (ig * xc)

    row = lax.broadcasted_iota(jnp.int32, (ts, 1), 0)
    sh = 1
    while sh < ts:
        a_sh = pltpu.roll(a, sh, 0)
        u_sh = pltpu.roll(u, sh, 0)
        valid = row >= sh
        u = jnp.where(valid, a * u_sh + u, u)
        a = jnp.where(valid, a * a_sh, a)
        sh *= 2
    hcur = u + a * h_sc[0:1, :]
    h_sc[...] = jnp.broadcast_to(hcur[ts - 1:ts, :], h_sc.shape)
    y_ref[...] = hcur * jax.nn.gelu(gate)


def _rglru(x2d, params, layer, batch, seq, ts):
    t, d = x2d.shape
    c = params[2].shape[-1]
    n_s = seq // ts
    tile = lambda b, i: (b * n_s + i, 0)
    kern = functools.partial(_rglru_kernel, ts=ts)
    return pl.pallas_call(
        kern,
        grid=(batch, n_s),
        in_specs=[pl.BlockSpec((ts, d), tile)] + [_lspec(a, layer) for a in params],
        out_specs=pl.BlockSpec((ts, c), tile),
        out_shape=jax.ShapeDtypeStruct((t, c), F32),
        scratch_shapes=[pltpu.VMEM((8, c), F32), pltpu.VMEM((8, c), F32)],
        compiler_params=_cparams(("parallel", "arbitrary")),
        name="rglru",
    )(x2d, *params)


def _memkv_kernel(mem_ref, g_ref, wk_ref, wv_ref, gk_ref, k_ref, v_ref):
    m = _rms(mem_ref[...], g_ref[...]).astype(BF16)
    kf = jnp.dot(m, wk_ref[...], preferred_element_type=F32)
    v_ref[...] = jnp.dot(m, wv_ref[...], preferred_element_type=F32).astype(BF16)
    gk = gk_ref[...]
    for hd in range(N_MEM_HEADS):
        sl = slice(hd * MEM_HEAD_DIM, (hd + 1) * MEM_HEAD_DIM)
        k_ref[:, sl] = _rms(kf[:, sl], gk).astype(BF16)


def _memkv(mem, params, layer):
    b, m_len, d = mem.shape
    dk = params[1].shape[-1]
    return pl.pallas_call(
        _memkv_kernel,
        grid=(b,),
        in_specs=[pl.BlockSpec((None, m_len, d), lambda i: (i, 0, 0))]
                 + [_lspec(a, layer) for a in params],
        out_specs=[pl.BlockSpec((None, m_len, dk), lambda i: (i, 0, 0)),
                   pl.BlockSpec((None, m_len, dk), lambda i: (i, 0, 0))],
        out_shape=[jax.ShapeDtypeStruct((b, m_len, dk), BF16),
                   jax.ShapeDtypeStruct((b, m_len, dk), BF16)],
        compiler_params=_cparams(("parallel",)),
        name="memkv",
    )(mem, *params)


def _post_kernel(x_ref, ya_ref, yr_ref, km_ref, vm_ref, ga_ref, gr_ref, woa_ref, wor_ref, gx_ref,
                 wmq_ref, gmq_ref, wmo_ref, gmoe_ref, wrt_ref, wrl_ref, brt_ref,
                 x2_ref, h2_ref, ri_ref, rw_ref, cnt_ref, cnt_sc, *, tm):
    step = pl.program_id(0)

    @pl.when(step == 0)
    def _():
        cnt_sc[...] = jnp.zeros(cnt_sc.shape, F32)

    ya = _rms(ya_ref[...], ga_ref[...]).astype(BF16)
    yr = _rms(yr_ref[...], gr_ref[...]).astype(BF16)
    x1 = (x_ref[...] + jnp.dot(ya, woa_ref[...], preferred_element_type=F32)
          + jnp.dot(yr, wor_ref[...], preferred_element_type=F32))

    hq = _rms(x1, gx_ref[...]).astype(BF16)
    qf = jnp.dot(hq, wmq_ref[...], preferred_element_type=F32)
    gmq = gmq_ref[...]
    o_parts = []
    for hd in range(N_MEM_HEADS):
        sl = slice(hd * MEM_HEAD_DIM, (hd + 1) * MEM_HEAD_DIM)
        qh = (_rms(qf[:, sl], gmq) * (MEM_HEAD_DIM ** -0.5)).astype(BF16)
        s = lax.dot_general(qh, km_ref[:, sl], (((1,), (1,)), ((), ())), preferred_element_type=F32)
        p = jnp.exp(s - jnp.max(s, axis=-1, keepdims=True))
        p = p / jnp.sum(p, axis=-1, keepdims=True)
        o_parts.append(jnp.dot(p.astype(BF16), vm_ref[:, sl], preferred_element_type=F32))
    o = jnp.concatenate(o_parts, axis=-1).astype(BF16)
    x2 = x1 + jnp.dot(o, wmo_ref[...], preferred_element_type=F32)
    x2_ref[...] = x2

    h2 = _rms(x2, gmoe_ref[...])
    h2_ref[...] = h2
    h_hi = h2.astype(BF16)
    h_lo = (h2 - h_hi.astype(F32)).astype(BF16)
    w_hi = wrt_ref[...]
    logits = (jnp.dot(h_hi, w_hi, preferred_element_type=F32)
              + jnp.dot(h_lo, w_hi, preferred_element_type=F32)
              + jnp.dot(h_hi, wrl_ref[...], preferred_element_type=F32)) + brt_ref[...]
    lane_i = lax.broadcasted_iota(jnp.int32, (tm, LANE), 1)
    lane = lane_i.astype(F32)
    big = float(LANE)
    is_g = lane_i < N_GROUPS
    gl = jnp.where(is_g, logits, NEG_BIG)
    gmax = jnp.max(gl, axis=-1, keepdims=True)
    gidx = jnp.min(jnp.where(gl == gmax, lane, big), axis=-1, keepdims=True)
    g_w = 1.0 / jnp.sum(jnp.where(is_g, jnp.exp(gl - gmax), 0.0), axis=-1, keepdims=True)
    lo = N_GROUPS + gidx * EXPERTS_PER_GROUP
    in_grp = (lane >= lo) & (lane < lo + EXPERTS_PER_GROUP)
    el = jnp.where(in_grp, logits, NEG_BIG)
    m1 = jnp.max(el, axis=-1, keepdims=True)
    i1 = jnp.min(jnp.where(el == m1, lane, big), axis=-1, keepdims=True)
    el2 = jnp.where(lane == i1, NEG_BIG, el)
    m2 = jnp.max(el2, axis=-1, keepdims=True)
    i2 = jnp.min(jnp.where(el2 == m2, lane, big), axis=-1, keepdims=True)
    zsum = jnp.sum(jnp.where(in_grp, jnp.exp(el - m1), 0.0), axis=-1, keepdims=True)
    p1 = 1.0 / zsum
    p2 = jnp.exp(m2 - m1) / zsum
    w1 = g_w * (p1 / (p1 + p2))
    w2 = g_w * (p2 / (p1 + p2))
    e1 = i1 - N_GROUPS
    e2 = i2 - N_GROUPS

    oh1 = lane == e1
    oh2 = lane == e2
    oh = jnp.where(oh1 | oh2, 1.0, 0.0)
    r_i = lax.broadcasted_iota(jnp.int32, (tm, tm), 0)
    c_i = lax.broadcasted_iota(jnp.int32, (tm, tm), 1)
    ltri = jnp.where(c_i < r_i, 1.0, 0.0).astype(BF16)
    cum = jnp.dot(ltri, oh.astype(BF16), preferred_element_type=F32) + cnt_sc[0:1, :]
    rank1 = jnp.sum(jnp.where(oh1, cum, 0.0), axis=-1, keepdims=True)
    rank2 = jnp.sum(jnp.where(oh2, cum, 0.0), axis=-1, keepdims=True)
    new_cnt = cnt_sc[0:1, :] + jnp.sum(oh, axis=0, keepdims=True)
    cnt_sc[...] = jnp.broadcast_to(new_cnt, cnt_sc.shape)
    cnt_ref[...] = jnp.broadcast_to(new_cnt, cnt_ref.shape)

    ri = jnp.where(lane_i == 0, e1, jnp.where(lane_i == 1, e2,
                   jnp.where(lane_i == 2, rank1, jnp.where(lane_i == 3, rank2, 0.0))))
    ri_ref[...] = ri.astype(jnp.int32)
    rw_ref[...] = jnp.where(lane_i == 0, w1, jnp.where(lane_i == 1, w2, 0.0))


def _post(x2d, ya, yr, km, vm, params, layer, seq, tm):
    t, d = x2d.shape
    n_s = seq // tm
    tile = lambda i: (i, 0)
    c2 = lambda i: (0, 0)
    mem_spec = lambda a: pl.BlockSpec((None,) + a.shape[1:], lambda i: (i // n_s, 0, 0))
    kern = functools.partial(_post_kernel, tm=tm)
    return pl.pallas_call(
        kern,
        grid=(t // tm,),
        in_specs=[pl.BlockSpec((tm, d), tile), pl.BlockSpec((tm, ya.shape[1]), tile),
                  pl.BlockSpec((tm, yr.shape[1]), tile), mem_spec(km), mem_spec(vm)]
                 + [_lspec(a, layer) for a in params],
        out_specs=[pl.BlockSpec((tm, d), tile), pl.BlockSpec((tm, d), tile),
                   pl.BlockSpec((tm, LANE), tile), pl.BlockSpec((tm, LANE), tile),
                   pl.BlockSpec((8, LANE), c2)],
        out_shape=[jax.ShapeDtypeStruct((t, d), F32), jax.ShapeDtypeStruct((t, d), F32),
                   jax.ShapeDtypeStruct((t, LANE), jnp.int32),
                   jax.ShapeDtypeStruct((t, LANE), F32),
                   jax.ShapeDtypeStruct((8, LANE), F32)],
        scratch_shapes=[pltpu.VMEM((8, LANE), F32)],
        compiler_params=_cparams(("arbitrary",)),
        name="post",
    )(x2d, ya, yr, km, vm, *params)


def _row_copy(src, dst, sem, src_row, dst_row):
    return pltpu.make_async_copy(src.at[pl.ds(src_row, 1)], dst.at[pl.ds(dst_row, 1)], sem)


def _dispatch_kernel(dest_ref, pend_ref, h_ref, xb_hbm, zero_sc, sem, zsem, *, tm, bm):
    i = pl.program_id(0)

    @pl.when(i == 0)
    def _():
        zero_sc[...] = jnp.zeros(zero_sc.shape, F32)

        def zero_copy(e):
            end = pend_ref[e]
            start = jnp.where(e == 0, 0, pend_ref[jnp.maximum(e - 1, 0)])
            dst = xb_hbm.at[pl.ds(pl.multiple_of(jnp.maximum(end - bm, 0), bm), bm)]
            return end > start, pltpu.make_async_copy(zero_sc, dst, zsem)

        def z_issue(e, c):
            nonempty, cp = zero_copy(e)

            @pl.when(nonempty)
            def _():
                cp.start()
            return c

        def z_drain(e, c):
            nonempty, cp = zero_copy(e)

            @pl.when(nonempty)
            def _():
                cp.wait()
            return c

        def tail_copy(b):
            dst = xb_hbm.at[pl.ds(pl.multiple_of(b * bm, bm), bm)]
            return pltpu.make_async_copy(zero_sc, dst, zsem)

        def t_issue(b, c):
            tail_copy(b).start()
            return c

        def t_drain(b, c):
            tail_copy(b).wait()
            return c

        first_unused = pend_ref[N_EXPERTS - 1] // bm
        n_blk = xb_hbm.shape[0] // bm
        lax.fori_loop(0, N_EXPERTS, z_issue, 0)
        lax.fori_loop(first_unused, n_blk, t_issue, 0)
        lax.fori_loop(0, N_EXPERTS, z_drain, 0)
        lax.fori_loop(first_unused, n_blk, t_drain, 0)

    n_tok = tm * pl.num_programs(0)
    base = tm * i
    for r in range(tm):
        _row_copy(h_ref, xb_hbm, sem, r, dest_ref[base + r]).start()
        _row_copy(h_ref, xb_hbm, sem, r, dest_ref[n_tok + base + r]).start()
    for _ in range(2):
        pltpu.make_async_copy(h_ref, xb_hbm.at[pl.ds(0, tm)], sem).wait()


def _dispatch(dest_flat, pad_end, h2, n_rows, tm, bm):
    t, d = h2.shape
    kern = functools.partial(_dispatch_kernel, tm=tm, bm=bm)
    grid_spec = pltpu.PrefetchScalarGridSpec(
        num_scalar_prefetch=2,
        grid=(t // tm,),
        in_specs=[pl.BlockSpec((tm, d), lambda i, de, pe: (i, 0))],
        out_specs=pl.BlockSpec(memory_space=pl.ANY),
        scratch_shapes=[pltpu.VMEM((bm, d), F32), pltpu.SemaphoreType.DMA(()),
                        pltpu.SemaphoreType.DMA(())],
    )
    return pl.pallas_call(
        kern,
        grid_spec=grid_spec,
        out_shape=jax.ShapeDtypeStruct((n_rows, d), F32),
        compiler_params=_cparams(("arbitrary",)),
        name="dispatch",
    )(dest_flat, pad_end, h2)


def _expert_kernel(blk_e_ref, nused_ref, x_ref, wg_ref, wu_ref, wd_ref, y_ref,
                   wg_sc, wu_sc, wd_sc):
    i = pl.program_id(0)
    used = i < nused_ref[0]

    @pl.when(used)
    def _():
        prev = jnp.maximum(i - 1, 0)

        @pl.when((i == 0) | (blk_e_ref[i] != blk_e_ref[prev]))
        def _():
            wg_sc[...] = wg_ref[...].astype(BF16)
            wu_sc[...] = wu_ref[...].astype(BF16)
            wd_sc[...] = wd_ref[...].astype(BF16)

        x = x_ref[...].astype(BF16)
        g = jnp.dot(x, wg_sc[...], preferred_element_type=F32)
        u = jnp.dot(x, wu_sc[...], preferred_element_type=F32)
        act = (g * jax.nn.sigmoid(g) * u).astype(BF16)
        y_ref[...] = jnp.dot(act, wd_sc[...], preferred_element_type=F32)

    @pl.when(jnp.logical_not(used))
    def _():
        y_ref[...] = jnp.zeros(y_ref.shape, F32)


def _experts(blk_e, nused, xb, w_gate, w_up, w_down, layer, bm):
    n_rows, d = xb.shape
    de = w_gate.shape[-1]
    wmap = lambda i, be, nu: (layer, be[i], 0, 0)
    grid_spec = pltpu.PrefetchScalarGridSpec(
        num_scalar_prefetch=2,
        grid=(n_rows // bm,),
        in_specs=[pl.BlockSpec((bm, d), lambda i, be, nu: (jnp.minimum(i, nu[0] - 1), 0)),
                  pl.BlockSpec((None, None, d, de), wmap),
                  pl.BlockSpec((None, None, d, de), wmap),
                  pl.BlockSpec((None, None, de, d), wmap)],
        out_specs=pl.BlockSpec((bm, d), lambda i, be, nu: (i, 0)),
        scratch_shapes=[pltpu.VMEM((d, de), BF16), pltpu.VMEM((d, de), BF16),
                        pltpu.VMEM((de, d), BF16)],
    )
    return pl.pallas_call(
        _expert_kernel,
        grid_spec=grid_spec,
        out_shape=jax.ShapeDtypeStruct((n_rows, d), F32),
        compiler_params=_cparams(("arbitrary",)),
        name="experts",
    )(blk_e, nused, xb, w_gate, w_up, w_down)


def _combine_kernel(dest_ref, x_ref, rw_ref, y_hbm, o_ref, buf, sem, *, tm):
    i = pl.program_id(0)
    slot = i % 2

    def gather(tile, to_slot):
        n_tok = tm * pl.num_programs(0)
        base = tm * tile
        for r in range(tm):
            for k in range(2):
                _row_copy(y_hbm, buf.at[to_slot, k], sem.at[to_slot],
                          dest_ref[k * n_tok + base + r], r).start()

    @pl.when(i == 0)
    def _():
        gather(0, 0)

    @pl.when(i + 1 < pl.num_programs(0))
    def _():
        gather(i + 1, 1 - slot)

    for k in range(2):
        pltpu.make_async_copy(y_hbm.at[pl.ds(0, tm)], buf.at[slot, k], sem.at[slot]).wait()
    rw = rw_ref[...]
    o_ref[...] = x_ref[...] + rw[:, 0:1] * buf[slot, 0] + rw[:, 1:2] * buf[slot, 1]


def _combine(dest_flat, x2, rw, yb, tm):
    t, d = x2.shape
    kern = functools.partial(_combine_kernel, tm=tm)
    grid_spec = pltpu.PrefetchScalarGridSpec(
        num_scalar_prefetch=1,
        grid=(t // tm,),
        in_specs=[pl.BlockSpec((tm, d), lambda i, de: (i, 0)),
                  pl.BlockSpec((tm, LANE), lambda i, de: (i, 0)),
                  pl.BlockSpec(memory_space=pl.ANY)],
        out_specs=pl.BlockSpec((tm, d), lambda i, de: (i, 0)),
        scratch_shapes=[pltpu.VMEM((2, 2, tm, d), F32), pltpu.SemaphoreType.DMA((2,))],
    )
    return pl.pallas_call(
        kern,
        grid_spec=grid_spec,
        out_shape=jax.ShapeDtypeStruct((t, d), F32),
        compiler_params=_cparams(("arbitrary",)),
        name="combine",
    )(dest_flat, x2, rw, yb)


def _rope_slot_tables(seq):
    inv = 1.0 / (ROPE_BASE ** (jnp.arange(0, QK_ROPE_DIM, 2, dtype=F32) / QK_ROPE_DIM))
    ang = jnp.arange(seq, dtype=F32)[:, None] * inv[None, :]
    cos, sin = jnp.cos(ang), jnp.sin(ang)
    zero = jnp.zeros_like(cos)
    cos_t = jnp.concatenate([cos, cos, zero, zero], axis=-1)
    sin_a = jnp.concatenate([-sin, zero, zero, zero], axis=-1)
    sin_b = jnp.concatenate([zero, sin, zero, zero], axis=-1)
    return cos_t, sin_a, sin_b


def _pick_tile(n, pref):
    t = min(n, pref)
    while n % t:
        t //= 2
    return t


def kernel(x, mem, mix_norm_g, w_in, q_lora_norm_g, kv_lora_norm_g, w_uq, w_ukv, att_q_norm_g, att_k_norm_g, conv_w, conv_b, w_rgate, b_rgate, w_igate, b_igate, lru_lambda, att_out_norm_g, rnn_out_norm_g, w_out, xattn_norm_g, mem_norm_g, w_mq, w_mk, w_mv, mem_q_norm_g, mem_k_norm_g, w_mo, moe_norm_g, w_router_group, b_router_group, w_router_expert, b_router_expert, w_exp_gate, w_exp_up, w_exp_down):
    batch, seq, d = x.shape
    depth = w_in.shape[0]
    t = batch * seq
    d_att = N_ATT_HEADS * LANE
    d_rnn = conv_w.shape[-1]
    tm = _pick_tile(seq, 256)
    tq = _pick_tile(seq, 512)
    tc = _pick_tile(seq, 512)
    bm = 256
    n_asg = 2 * t
    n_rows = (-(-n_asg // bm) + N_EXPERTS) * bm
    n_blk = n_rows // bm

    cos_t, sin_a, sin_b = _rope_slot_tables(seq)
    xc = x.reshape(t, d)
    row = lambda a: a[:, None, :]

    o_pe = Q_LORA_RANK + KV_LORA_RANK + QK_ROPE_DIM
    w_att = jnp.concatenate([w_in[:, :, :o_pe], jnp.zeros((depth, d, LANE - QK_ROPE_DIM), F32)],
                            axis=2).astype(BF16)
    w_rnn = w_in[:, :, o_pe:].astype(BF16)
    wuq = jnp.pad(w_uq.reshape(depth, Q_LORA_RANK, N_ATT_HEADS, QK_HEAD_DIM),
                  ((0, 0), (0, 0), (0, 0), (0, HEAD_SLOT - QK_HEAD_DIM))
                  ).reshape(depth, Q_LORA_RANK, N_ATT_HEADS * HEAD_SLOT).astype(BF16)
    wukv = w_ukv.reshape(depth, KV_LORA_RANK, N_ATT_HEADS, 2 * LANE)
    wuk = wukv[..., :LANE].reshape(depth, KV_LORA_RANK, d_att).astype(BF16)
    wuv = wukv[..., LANE:].reshape(depth, KV_LORA_RANK, d_att).astype(BF16)
    slot_pad = jnp.zeros((depth, HEAD_SLOT - QK_HEAD_DIM), F32)
    qkv_params = (row(mix_norm_g), w_att, row(q_lora_norm_g), row(kv_lora_norm_g), wuq, wuk, wuv,
                  row(jnp.concatenate([att_q_norm_g, slot_pad], axis=1)),
                  row(jnp.concatenate([att_k_norm_g, slot_pad], axis=1)))
    rnn_params = (row(mix_norm_g), w_rnn, conv_w, row(conv_b), w_rgate.astype(BF16), row(b_rgate),
                  w_igate.astype(BF16), row(b_igate), row(lru_lambda))
    mem_params = (row(mem_norm_g), w_mk.astype(BF16), w_mv.astype(BF16), row(mem_k_norm_g))
    n_pad = LANE - N_GROUPS - N_EXPERTS
    wrt = jnp.concatenate([w_router_group, w_router_expert, jnp.zeros((depth, d, n_pad), F32)], axis=2)
    brt = jnp.concatenate([b_router_group, b_router_expert, jnp.zeros((depth, n_pad), F32)], axis=1)
    wrt_hi = wrt.astype(BF16)
    wrt_lo = (wrt - wrt_hi.astype(F32)).astype(BF16)
    post_params = (row(att_out_norm_g), row(rnn_out_norm_g), w_out[:, :d_att].astype(BF16),
                   w_out[:, d_att:].astype(BF16), row(xattn_norm_g), w_mq.astype(BF16),
                   row(mem_q_norm_g), w_mo.astype(BF16), row(moe_norm_g), wrt_hi, wrt_lo, row(brt))

    for l in range(depth):
        q, k, v = _qkv(xc, qkv_params, l, cos_t, sin_a, sin_b, seq, tq)
        y_att = _attention(q.reshape(batch, seq, -1), k.reshape(batch, seq, -1),
                           v.reshape(batch, seq, -1), tc).reshape(t, d_att)
        y_rnn = _rglru(xc, rnn_params, l, batch, seq, tm)
        km, vm = _memkv(mem, mem_params, l)
        x2, h2, ri, rw, cnt = _post(xc, y_att, y_rnn, km, vm, post_params, l, seq, tm)

        counts = cnt[0, :N_EXPERTS].astype(jnp.int32)
        padded = (counts + bm - 1) // bm * bm
        pad_end = jnp.cumsum(padded)
        pad_start = pad_end - padded
        ri_t = ri[:, 0:4].T
        dest = (pad_start[ri_t[0:2]] + ri_t[2:4]).reshape(n_asg)
        blk_start = jnp.arange(n_blk, dtype=jnp.int32) * bm
        blk_e = jnp.minimum(jnp.sum((pad_end[None, :] <= blk_start[:, None]).astype(jnp.int32), axis=1),
                            N_EXPERTS - 1)
        nused = pad_end[-1:] // bm

        xb = _dispatch(dest, pad_end, h2, n_rows, tm, bm)
        yb = _experts(blk_e, nused, xb, w_exp_gate, w_exp_up, w_exp_down, l, bm)
        xc = _combine(dest, x2, rw, yb, tm)

    return xc.reshape(batch, seq, d)
```

```python
import functools

import jax
import jax.numpy as jnp
from jax import lax
from jax.experimental import pallas as pl
from jax.experimental.pallas import tpu as pltpu

EPS = 1e-6
N_ATT_HEADS = 8
QK_NOPE_DIM = 128
QK_ROPE_DIM = 64
QK_HEAD_DIM = QK_NOPE_DIM + QK_ROPE_DIM
Q_LORA_RANK = 512
KV_LORA_RANK = 256
ROPE_BASE = 10000.0
N_RNN_BLOCKS = 8
CONV_WIDTH = 4
LRU_C = 8.0
N_MEM_HEADS = 4
MEM_HEAD_DIM = 128
N_GROUPS = 4
EXPERTS_PER_GROUP = 8
N_EXPERTS = N_GROUPS * EXPERTS_PER_GROUP

LANE = 128
HEAD_SLOT = 2 * LANE
NEG_BIG = -1e30
LOG2_E = 1.4426950408889634
ATTN_UNROLL = 4

VMEM_LIMIT = 56 * 1024 * 1024

F32 = jnp.float32
BF16 = jnp.bfloat16


def _cparams(sem):
    return pltpu.CompilerParams(dimension_semantics=sem, vmem_limit_bytes=VMEM_LIMIT)


def _rms(x, g):
    return x * lax.rsqrt(jnp.mean(x * x, axis=-1, keepdims=True) + EPS) * g


def _rope_slot(r, cos_t, sin_a, sin_b):
    return r * cos_t + pltpu.roll(r, 96, 1) * sin_a + pltpu.roll(r, 32, 1) * sin_b


def _lspec(a, layer):
    return pl.BlockSpec((None,) + a.shape[1:], lambda *_: (layer,) + (0,) * (a.ndim - 1))


def _qkv_kernel(x_ref, gmix_ref, win_ref, gql_ref, gkvl_ref, wuq_ref, wuk_ref, wuv_ref,
                gq_ref, gk_ref, cos_ref, sina_ref, sinb_ref, q_ref, k_ref, v_ref):
    h = _rms(x_ref[...], gmix_ref[...]).astype(BF16)
    z = jnp.dot(h, win_ref[...], preferred_element_type=F32)
    o_kv = Q_LORA_RANK
    o_pe = o_kv + KV_LORA_RANK
    o_xr = o_pe + LANE

    cos_t = cos_ref[...]
    sin_a = sina_ref[...]
    sin_b = sinb_ref[...]
    inv_d = 1.0 / QK_HEAD_DIM

    cq = _rms(z[:, :o_kv], gql_ref[...]).astype(BF16)
    qf = jnp.dot(cq, wuq_ref[...], preferred_element_type=F32)
    gq = gq_ref[...]
    for hd in range(N_ATT_HEADS):
        slot = qf[:, hd * HEAD_SLOT:(hd + 1) * HEAD_SLOT]
        ss = jnp.sum(slot * slot, axis=-1, keepdims=True) * inv_d
        sn = slot * lax.rsqrt(ss + EPS) * gq
        q_ref[:, hd * HEAD_SLOT:hd * HEAD_SLOT + LANE] = sn[:, :LANE].astype(BF16)
        q_ref[:, hd * HEAD_SLOT + LANE:(hd + 1) * HEAD_SLOT] = _rope_slot(
            sn[:, LANE:], cos_t, sin_a, sin_b).astype(BF16)

    ckv = _rms(z[:, o_kv:o_pe], gkvl_ref[...]).astype(BF16)
    kn = jnp.dot(ckv, wuk_ref[...], preferred_element_type=F32)
    vv = jnp.dot(ckv, wuv_ref[...], preferred_element_type=F32).astype(BF16)
    ones = jnp.ones((vv.shape[0], LANE), BF16)
    for hd in range(N_ATT_HEADS):
        v_ref[:, hd * HEAD_SLOT:hd * HEAD_SLOT + LANE] = vv[:, hd * LANE:(hd + 1) * LANE]
        v_ref[:, hd * HEAD_SLOT + LANE:(hd + 1) * HEAD_SLOT] = ones
    kpe = z[:, o_pe:o_xr]
    pe_ss = jnp.sum(kpe * kpe, axis=-1, keepdims=True)
    gk = gk_ref[...]
    kpe_rot = _rope_slot(kpe * gk[:, LANE:], cos_t, sin_a, sin_b)
    for hd in range(N_ATT_HEADS):
        kh = kn[:, hd * LANE:(hd + 1) * LANE]
        ss = (jnp.sum(kh * kh, axis=-1, keepdims=True) + pe_ss) * inv_d
        inv = lax.rsqrt(ss + EPS)
        k_ref[:, hd * HEAD_SLOT:hd * HEAD_SLOT + LANE] = (kh * inv * gk[:, :LANE]).astype(BF16)
        k_ref[:, hd * HEAD_SLOT + LANE:(hd + 1) * HEAD_SLOT] = (kpe_rot * inv).astype(BF16)


def _qkv(x2d, params, layer, cos_t, sin_a, sin_b, seq, tm):
    t, d = x2d.shape
    n_s = seq // tm
    tile = lambda i: (i, 0)
    pos = lambda i: (i % n_s, 0)
    slots = N_ATT_HEADS * HEAD_SLOT
    return pl.pallas_call(
        _qkv_kernel,
        grid=(t // tm,),
        in_specs=[pl.BlockSpec((tm, d), tile)] + [_lspec(a, layer) for a in params]
                 + [pl.BlockSpec((tm, LANE), pos)] * 3,
        out_specs=[pl.BlockSpec((tm, slots), tile)] * 3,
        out_shape=[jax.ShapeDtypeStruct((t, slots), BF16)] * 3,
        compiler_params=_cparams(("parallel",)),
        name="qkv",
    )(x2d, *params, cos_t, sin_a, sin_b)


def _attn_pairs(nq):
    full = [(i, j) for i in range(nq) for j in range(i)]
    diag = [(i, i) for i in range(nq)]
    return full + diag, len(full)


def _attn_kernel(qi_ref, kj_ref, q_ref, k_ref, v_ref, o_ref, acc_all, m_all,
                 s0, s1, p0, p1, a0, a1, *, tc, n_pairs, n_full):
    m_all[...] = jnp.full(m_all.shape, NEG_BIG, F32)
    acc_all[...] = jnp.zeros(acc_all.shape, F32)
    sb, pb, ab = (s0, s1), (p0, p1), (a0, a1)

    def rows(idx):
        return pl.ds(pl.multiple_of(idx * tc, tc), tc)

    def stage_a(n, par):
        sb[par][...] = lax.dot_general(q_ref[rows(qi_ref[n]), :], k_ref[rows(kj_ref[n]), :],
                                       (((1,), (1,)), ((), ())), preferred_element_type=F32)

    def stage_b(n, par, masked):
        r = rows(qi_ref[n])
        s = sb[par][...]
        if masked:
            rpos = lax.broadcasted_iota(jnp.int32, (tc, tc), 0)
            cpos = lax.broadcasted_iota(jnp.int32, (tc, tc), 1)
            s = jnp.where(cpos <= rpos, s, NEG_BIG)
        m_prev = m_all[r, :]
        m_new = jnp.maximum(m_prev, jnp.max(s, axis=-1, keepdims=True))
        pb[par][...] = jnp.exp2(s - jnp.tile(m_new, (1, tc // LANE))).astype(BF16)
        ab[par][...] = jnp.exp2(m_prev - m_new)
        m_all[r, :] = m_new

    def stage_c(n, par):
        r = rows(qi_ref[n])
        pv = jnp.dot(pb[par][...], v_ref[rows(kj_ref[n]), :], preferred_element_type=F32)
        acc_all[r, :] = jnp.tile(ab[par][...], (1, 2)) * acc_all[r, :] + pv

    def sub(n, par, do_a, do_b, do_c, masked):
        if do_a:
            stage_a(n + 2, par)
        if do_b:
            stage_b(n + 1, 1 - par, masked)
        if do_c:
            stage_c(n, par)

    runs = []
    for n in range(-2, n_pairs):
        flags = (n + 2 < n_pairs, 0 <= n + 1 < n_pairs, n >= 0, n + 1 >= n_full)
        if runs and runs[-1][1] == flags:
            runs[-1][2] += 1
        else:
            runs.append([n, flags, 1])
    for n_start, flags, count in runs:
        par0 = n_start % 2
        n_loop = count // ATTN_UNROLL if all(flags[:3]) else 0
        if n_loop:
            def body(u, carry, n_start=n_start, flags=flags, par0=par0):
                n = n_start + ATTN_UNROLL * u
                for k in range(ATTN_UNROLL):
                    sub(n + k, (par0 + k) % 2, *flags)
                return carry

            lax.fori_loop(0, n_loop, body, 0)
        for n in range(n_start + ATTN_UNROLL * n_loop, n_start + count):
            sub(n, n % 2, *flags)

    for it in range(acc_all.shape[0] // tc):
        r = slice(it * tc, (it + 1) * tc)
        o_ref[r, :] = acc_all[r, :LANE] / acc_all[r, LANE:]


def _attention(q, k, v, tc):
    b, s, _ = q.shape
    nq = s // tc
    pairs, n_full = _attn_pairs(nq)
    qi = jnp.asarray([p[0] for p in pairs], jnp.int32)
    kj = jnp.asarray([p[1] for p in pairs], jnp.int32)
    kern = functools.partial(_attn_kernel, tc=tc, n_pairs=len(pairs), n_full=n_full)
    head_blk = lambda bi, h, qi_r, kj_r: (bi, 0, h)
    grid_spec = pltpu.PrefetchScalarGridSpec(
        num_scalar_prefetch=2,
        grid=(b, N_ATT_HEADS),
        in_specs=[pl.BlockSpec((None, s, HEAD_SLOT), head_blk),
                  pl.BlockSpec((None, s, HEAD_SLOT), head_blk),
                  pl.BlockSpec((None, s, HEAD_SLOT), head_blk)],
        out_specs=pl.BlockSpec((None, s, LANE), head_blk),
        scratch_shapes=[pltpu.VMEM((s, HEAD_SLOT), F32), pltpu.VMEM((s, LANE), F32),
                        pltpu.VMEM((tc, tc), F32), pltpu.VMEM((tc, tc), F32),
                        pltpu.VMEM((tc, tc), BF16), pltpu.VMEM((tc, tc), BF16),
                        pltpu.VMEM((tc, LANE), F32), pltpu.VMEM((tc, LANE), F32)],
    )
    return pl.pallas_call(
        kern,
        grid_spec=grid_spec,
        out_shape=jax.ShapeDtypeStruct((b, s, N_ATT_HEADS * LANE), F32),
        compiler_params=_cparams(("parallel", "parallel")),
        name="attn",
    )(qi, kj, q, k, v)


def _shift_rows(x, sh, before):
    n, c = x.shape
    rot = pltpu.roll(x.reshape(n // 8, 8, c), sh, 1)
    prev = jnp.concatenate([before[None], rot[:-1]], axis=0)
    row8 = lax.broadcasted_iota(jnp.int32, (1, 8, 1), 1)
    return jnp.where(row8 < sh, prev, rot).reshape(n, c)


def _rglru_kernel(x_ref, gmix_ref, win_ref, cw_ref, cb_ref, wr_ref, br_ref, wi_ref, bi_ref, lam_ref,
                  y_ref, tail_sc, h_sc, *, ts):
    i = pl.program_id(1)

    @pl.when(i == 0)
    def _():
        tail_sc[...] = jnp.zeros(tail_sc.shape, F32)
        h_sc[...] = jnp.zeros(h_sc.shape, F32)

    c = y_ref.shape[-1]
    hn = _rms(x_ref[...], gmix_ref[...]).astype(BF16)
    z = jnp.dot(hn, win_ref[...], preferred_element_type=F32)
    xr = z[:, :c]
    gate = z[:, c:]

    cw = cw_ref[...]
    tail = tail_sc[...]
    xc = cb_ref[...] + xr * cw[CONV_WIDTH - 1:CONV_WIDTH, :]
    for sh in range(1, CONV_WIDTH):
        shifted = _shift_rows(xr, sh, pltpu.roll(tail, sh, 0))
        xc = xc + shifted * cw[CONV_WIDTH - 1 - sh:CONV_WIDTH - sh, :]
    tail_sc[...] = xr[ts - 8:, :]

    blk = c // N_RNN_BLOCKS
    xcb = xc.astype(BF16)
    r_parts = []
    i_parts = []
    for n in range(N_RNN_BLOCKS):
        xb = xcb[:, n * blk:(n + 1) * blk]
        r_parts.append(jnp.dot(xb, wr_ref[n], preferred_element_type=F32))
        i_parts.append(jnp.dot(xb, wi_ref[n], preferred_element_type=F32))
    r = 0.5 * jnp.tanh(0.5 * (jnp.concatenate(r_parts, axis=-1) + br_ref[...])) + 0.5
    ig = 0.5 * jnp.tanh(0.5 * (jnp.concatenate(i_parts, axis=-1) + bi_ref[...])) + 0.5

    nlam = -lam_ref[...]
    softplus = jnp.maximum(nlam, 0.0) + jnp.log(1.0 + jnp.exp(-jnp.abs(nlam)))
    log_a = (-LRU_C * r) * softplus
    a = jnp.exp(log_a)
    u = jnp.sqrt(1.0 - a * a) * (ig * xc)

    ones8 = jnp.ones((8, c), F32)
    sh = 1
    while sh < ts:
        if sh < 8:
            u = a * _shift_rows(u, sh, jnp.zeros((8, c), F32)) + u
            a = a * _shift_rows(a, sh, ones8)
        else:
            u = jnp.concatenate([u[:sh], a[sh:] * u[:ts - sh] + u[sh:]], axis=0)
            a = jnp.concatenate([a[:sh], a[sh:] * a[:ts - sh]], axis=0)
        sh *= 2
    hcur = u + a * h_sc[0:1, :]
    h_sc[...] = jnp.broadcast_to(hcur[ts - 1:ts, :], h_sc.shape)
    y_ref[...] = hcur * jax.nn.gelu(gate)


def _rglru(x2d, params, layer, batch, seq, ts):
    t, d = x2d.shape
    c = params[2].shape[-1]
    n_s = seq // ts
    tile = lambda b, i: (b * n_s + i, 0)
    kern = functools.partial(_rglru_kernel, ts=ts)
    return pl.pallas_call(
        kern,
        grid=(batch, n_s),
        in_specs=[pl.BlockSpec((ts, d), tile)] + [_lspec(a, layer) for a in params],
        out_specs=pl.BlockSpec((ts, c), tile),
        out_shape=jax.ShapeDtypeStruct((t, c), F32),
        scratch_shapes=[pltpu.VMEM((8, c), F32), pltpu.VMEM((8, c), F32)],
        compiler_params=_cparams(("parallel", "arbitrary")),
        name="rglru",
    )(x2d, *params)


def _memkv_kernel(mem_ref, g_ref, wk_ref, wv_ref, gk_ref, k_ref, v_ref):
    m = _rms(mem_ref[...], g_ref[...]).astype(BF16)
    kf = jnp.dot(m, wk_ref[...], preferred_element_type=F32)
    v_ref[...] = jnp.dot(m, wv_ref[...], preferred_element_type=F32).astype(BF16)
    gk = gk_ref[...]
    for hd in range(N_MEM_HEADS):
        sl = slice(hd * MEM_HEAD_DIM, (hd + 1) * MEM_HEAD_DIM)
        k_ref[:, sl] = _rms(kf[:, sl], gk).astype(BF16)


def _memkv(mem, params, layer):
    b, m_len, d = mem.shape
    dk = params[1].shape[-1]
    return pl.pallas_call(
        _memkv_kernel,
        grid=(b,),
        in_specs=[pl.BlockSpec((None, m_len, d), lambda i: (i, 0, 0))]
                 + [_lspec(a, layer) for a in params],
        out_specs=[pl.BlockSpec((None, m_len, dk), lambda i: (i, 0, 0)),
                   pl.BlockSpec((None, m_len, dk), lambda i: (i, 0, 0))],
        out_shape=[jax.ShapeDtypeStruct((b, m_len, dk), BF16),
                   jax.ShapeDtypeStruct((b, m_len, dk), BF16)],
        compiler_params=_cparams(("parallel",)),
        name="memkv",
    )(mem, *params)


def _post_kernel(x_ref, ya_ref, yr_ref, km_ref, vm_ref, ga_ref, gr_ref, woa_ref, wor_ref, gx_ref,
                 wmq_ref, gmq_ref, wmo_ref, gmoe_ref, wrt_ref, wrl_ref, brt_ref,
                 x2_ref, h2_ref, ri_ref, rw_ref, cnt_ref, cnt_sc, *, tm):
    step = pl.program_id(0)

    @pl.when(step == 0)
    def _():
        cnt_sc[...] = jnp.zeros(cnt_sc.shape, F32)

    ya = _rms(ya_ref[...], ga_ref[...]).astype(BF16)
    yr = _rms(yr_ref[...], gr_ref[...]).astype(BF16)
    x1 = (x_ref[...] + jnp.dot(ya, woa_ref[...], preferred_element_type=F32)
          + jnp.dot(yr, wor_ref[...], preferred_element_type=F32))

    hq = _rms(x1, gx_ref[...]).astype(BF16)
    qf = jnp.dot(hq, wmq_ref[...], preferred_element_type=F32)
    gmq = gmq_ref[...]
    o_parts = []
    for hd in range(N_MEM_HEADS):
        sl = slice(hd * MEM_HEAD_DIM, (hd + 1) * MEM_HEAD_DIM)
        qh = (_rms(qf[:, sl], gmq) * (MEM_HEAD_DIM ** -0.5)).astype(BF16)
        s = lax.dot_general(qh, km_ref[:, sl], (((1,), (1,)), ((), ())), preferred_element_type=F32)
        p = jnp.exp(s - jnp.max(s, axis=-1, keepdims=True))
        p = p / jnp.sum(p, axis=-1, keepdims=True)
        o_parts.append(jnp.dot(p.astype(BF16), vm_ref[:, sl], preferred_element_type=F32))
    o = jnp.concatenate(o_parts, axis=-1).astype(BF16)
    x2 = x1 + jnp.dot(o, wmo_ref[...], preferred_element_type=F32)
    x2_ref[...] = x2

    h2 = _rms(x2, gmoe_ref[...])
    h2_ref[...] = h2
    h_hi = h2.astype(BF16)
    h_lo = (h2 - h_hi.astype(F32)).astype(BF16)
    w_hi = wrt_ref[...]
    logits = (jnp.dot(h_hi, w_hi, preferred_element_type=F32)
              + jnp.dot(h_lo, w_hi, preferred_element_type=F32)
              + jnp.dot(h_hi, wrl_ref[...], preferred_element_type=F32)) + brt_ref[...]
    lane_i = lax.broadcasted_iota(jnp.int32, (tm, LANE), 1)
    lane = lane_i.astype(F32)
    big = float(LANE)
    is_g = lane_i < N_GROUPS
    gl = jnp.where(is_g, logits, NEG_BIG)
    gmax = jnp.max(gl, axis=-1, keepdims=True)
    gidx = jnp.min(jnp.where(gl == gmax, lane, big), axis=-1, keepdims=True)
    g_w = 1.0 / jnp.sum(jnp.where(is_g, jnp.exp(gl - gmax), 0.0), axis=-1, keepdims=True)
    lo = N_GROUPS + gidx * EXPERTS_PER_GROUP
    in_grp = (lane >= lo) & (lane < lo + EXPERTS_PER_GROUP)
    el = jnp.where(in_grp, logits, NEG_BIG)
    m1 = jnp.max(el, axis=-1, keepdims=True)
    i1 = jnp.min(jnp.where(el == m1, lane, big), axis=-1, keepdims=True)
    el2 = jnp.where(lane == i1, NEG_BIG, el)
    m2 = jnp.max(el2, axis=-1, keepdims=True)
    i2 = jnp.min(jnp.where(el2 == m2, lane, big), axis=-1, keepdims=True)
    zsum = jnp.sum(jnp.where(in_grp, jnp.exp(el - m1), 0.0), axis=-1, keepdims=True)
    p1 = 1.0 / zsum
    p2 = jnp.exp(m2 - m1) / zsum
    w1 = g_w * (p1 / (p1 + p2))
    w2 = g_w * (p2 / (p1 + p2))
    e1 = i1 - N_GROUPS
    e2 = i2 - N_GROUPS

    oh1 = lane == e1
    oh2 = lane == e2
    oh = jnp.where(oh1 | oh2, 1.0, 0.0)
    r_i = lax.broadcasted_iota(jnp.int32, (tm, tm), 0)
    c_i = lax.broadcasted_iota(jnp.int32, (tm, tm), 1)
    ltri = jnp.where(c_i < r_i, 1.0, 0.0).astype(BF16)
    cum = jnp.dot(ltri, oh.astype(BF16), preferred_element_type=F32) + cnt_sc[0:1, :]
    rank1 = jnp.sum(jnp.where(oh1, cum, 0.0), axis=-1, keepdims=True)
    rank2 = jnp.sum(jnp.where(oh2, cum, 0.0), axis=-1, keepdims=True)
    new_cnt = cnt_sc[0:1, :] + jnp.sum(oh, axis=0, keepdims=True)
    cnt_sc[...] = jnp.broadcast_to(new_cnt, cnt_sc.shape)
    cnt_ref[...] = jnp.broadcast_to(new_cnt, cnt_ref.shape)

    ri = jnp.where(lane_i == 0, e1, jnp.where(lane_i == 1, e2,
                   jnp.where(lane_i == 2, rank1, jnp.where(lane_i == 3, rank2, 0.0))))
    ri_ref[...] = ri.astype(jnp.int32)
    rw_ref[...] = jnp.where(lane_i == 0, w1, jnp.where(lane_i == 1, w2, 0.0))


def _post(x2d, ya, yr, km, vm, params, layer, seq, tm):
    t, d = x2d.shape
    n_s = seq // tm
    tile = lambda i: (i, 0)
    c2 = lambda i: (0, 0)
    mem_spec = lambda a: pl.BlockSpec((None,) + a.shape[1:], lambda i: (i // n_s, 0, 0))
    kern = functools.partial(_post_kernel, tm=tm)
    return pl.pallas_call(
        kern,
        grid=(t // tm,),
        in_specs=[pl.BlockSpec((tm, d), tile), pl.BlockSpec((tm, ya.shape[1]), tile),
                  pl.BlockSpec((tm, yr.shape[1]), tile), mem_spec(km), mem_spec(vm)]
                 + [_lspec(a, layer) for a in params],
        out_specs=[pl.BlockSpec((tm, d), tile), pl.BlockSpec((tm, d), tile),
                   pl.BlockSpec((tm, LANE), tile), pl.BlockSpec((tm, LANE), tile),
                   pl.BlockSpec((8, LANE), c2)],
        out_shape=[jax.ShapeDtypeStruct((t, d), F32), jax.ShapeDtypeStruct((t, d), F32),
                   jax.ShapeDtypeStruct((t, LANE), jnp.int32),
                   jax.ShapeDtypeStruct((t, LANE), F32),
                   jax.ShapeDtypeStruct((8, LANE), F32)],
        scratch_shapes=[pltpu.VMEM((8, LANE), F32)],
        compiler_params=_cparams(("arbitrary",)),
        name="post",
    )(x2d, ya, yr, km, vm, *params)


def _row_copy(src, dst, sem, src_row, dst_row):
    return pltpu.make_async_copy(src.at[pl.ds(src_row, 1)], dst.at[pl.ds(dst_row, 1)], sem)


def _dispatch_kernel(dest_ref, pend_ref, h_ref, xb_hbm, zero_sc, sem, zsem, *, tm, bm):
    i = pl.program_id(0)

    @pl.when(i == 0)
    def _():
        zero_sc[...] = jnp.zeros(zero_sc.shape, F32)

        def zero_copy(e):
            end = pend_ref[e]
            start = jnp.where(e == 0, 0, pend_ref[jnp.maximum(e - 1, 0)])
            dst = xb_hbm.at[pl.ds(pl.multiple_of(jnp.maximum(end - bm, 0), bm), bm)]
            return end > start, pltpu.make_async_copy(zero_sc, dst, zsem)

        def z_issue(e, c):
            nonempty, cp = zero_copy(e)

            @pl.when(nonempty)
            def _():
                cp.start()
            return c

        def z_drain(e, c):
            nonempty, cp = zero_copy(e)

            @pl.when(nonempty)
            def _():
                cp.wait()
            return c

        def tail_copy(b):
            dst = xb_hbm.at[pl.ds(pl.multiple_of(b * bm, bm), bm)]
            return pltpu.make_async_copy(zero_sc, dst, zsem)

        def t_issue(b, c):
            tail_copy(b).start()
            return c

        def t_drain(b, c):
            tail_copy(b).wait()
            return c

        first_unused = pend_ref[N_EXPERTS - 1] // bm
        n_blk = xb_hbm.shape[0] // bm
        lax.fori_loop(0, N_EXPERTS, z_issue, 0)
        lax.fori_loop(first_unused, n_blk, t_issue, 0)
        lax.fori_loop(0, N_EXPERTS, z_drain, 0)
        lax.fori_loop(first_unused, n_blk, t_drain, 0)

    n_tok = tm * pl.num_programs(0)
    base = tm * i
    for r in range(tm):
        _row_copy(h_ref, xb_hbm, sem, r, dest_ref[base + r]).start()
        _row_copy(h_ref, xb_hbm, sem, r, dest_ref[n_tok + base + r]).start()
    for _ in range(2):
        pltpu.make_async_copy(h_ref, xb_hbm.at[pl.ds(0, tm)], sem).wait()


def _dispatch(dest_flat, pad_end, h2, n_rows, tm, bm):
    t, d = h2.shape
    kern = functools.partial(_dispatch_kernel, tm=tm, bm=bm)
    grid_spec = pltpu.PrefetchScalarGridSpec(
        num_scalar_prefetch=2,
        grid=(t // tm,),
        in_specs=[pl.BlockSpec((tm, d), lambda i, de, pe: (i, 0))],
        out_specs=pl.BlockSpec(memory_space=pl.ANY),
        scratch_shapes=[pltpu.VMEM((bm, d), F32), pltpu.SemaphoreType.DMA(()),
                        pltpu.SemaphoreType.DMA(())],
    )
    return pl.pallas_call(
        kern,
        grid_spec=grid_spec,
        out_shape=jax.ShapeDtypeStruct((n_rows, d), F32),
        compiler_params=_cparams(("arbitrary",)),
        name="dispatch",
    )(dest_flat, pad_end, h2)


def _expert_kernel(blk_e_ref, nused_ref, x_ref, wg_ref, wu_ref, wd_ref, y_ref,
                   wg_sc, wu_sc, wd_sc):
    i = pl.program_id(0)
    used = i < nused_ref[0]

    @pl.when(used)
    def _():
        prev = jnp.maximum(i - 1, 0)

        @pl.when((i == 0) | (blk_e_ref[i] != blk_e_ref[prev]))
        def _():
            wg_sc[...] = wg_ref[...].astype(BF16)
            wu_sc[...] = wu_ref[...].astype(BF16)
            wd_sc[...] = wd_ref[...].astype(BF16)

        x = x_ref[...].astype(BF16)
        g = jnp.dot(x, wg_sc[...], preferred_element_type=F32)
        u = jnp.dot(x, wu_sc[...], preferred_element_type=F32)
        act = (g * jax.nn.sigmoid(g) * u).astype(BF16)
        y_ref[...] = jnp.dot(act, wd_sc[...], preferred_element_type=F32)

    @pl.when(jnp.logical_not(used))
    def _():
        y_ref[...] = jnp.zeros(y_ref.shape, F32)


def _experts(blk_e, nused, xb, w_gate, w_up, w_down, layer, bm):
    n_rows, d = xb.shape
    de = w_gate.shape[-1]
    wmap = lambda i, be, nu: (layer, be[i], 0, 0)
    grid_spec = pltpu.PrefetchScalarGridSpec(
        num_scalar_prefetch=2,
        grid=(n_rows // bm,),
        in_specs=[pl.BlockSpec((bm, d), lambda i, be, nu: (jnp.minimum(i, nu[0] - 1), 0)),
                  pl.BlockSpec((None, None, d, de), wmap),
                  pl.BlockSpec((None, None, d, de), wmap),
                  pl.BlockSpec((None, None, de, d), wmap)],
        out_specs=pl.BlockSpec((bm, d), lambda i, be, nu: (i, 0)),
        scratch_shapes=[pltpu.VMEM((d, de), BF16), pltpu.VMEM((d, de), BF16),
                        pltpu.VMEM((de, d), BF16)],
    )
    return pl.pallas_call(
        _expert_kernel,
        grid_spec=grid_spec,
        out_shape=jax.ShapeDtypeStruct((n_rows, d), F32),
        compiler_params=_cparams(("arbitrary",)),
        name="experts",
    )(blk_e, nused, xb, w_gate, w_up, w_down)


def _combine_kernel(dest_ref, x_ref, rw_ref, y_hbm, o_ref, buf, sem, *, tm):
    i = pl.program_id(0)
    slot = i % 2

    def gather(tile, to_slot):
        n_tok = tm * pl.num_programs(0)
        base = tm * tile
        for r in range(tm):
            for k in range(2):
                _row_copy(y_hbm, buf.at[to_slot, k], sem.at[to_slot],
                          dest_ref[k * n_tok + base + r], r).start()

    @pl.when(i == 0)
    def _():
        gather(0, 0)

    @pl.when(i + 1 < pl.num_programs(0))
    def _():
        gather(i + 1, 1 - slot)

    for k in range(2):
        pltpu.make_async_copy(y_hbm.at[pl.ds(0, tm)], buf.at[slot, k], sem.at[slot]).wait()
    rw = rw_ref[...]
    o_ref[...] = x_ref[...] + rw[:, 0:1] * buf[slot, 0] + rw[:, 1:2] * buf[slot, 1]


def _combine(dest_flat, x2, rw, yb, tm):
    t, d = x2.shape
    kern = functools.partial(_combine_kernel, tm=tm)
    grid_spec = pltpu.PrefetchScalarGridSpec(
        num_scalar_prefetch=1,
        grid=(t // tm,),
        in_specs=[pl.BlockSpec((tm, d), lambda i, de: (i, 0)),
                  pl.BlockSpec((tm, LANE), lambda i, de: (i, 0)),
                  pl.BlockSpec(memory_space=pl.ANY)],
        out_specs=pl.BlockSpec((tm, d), lambda i, de: (i, 0)),
        scratch_shapes=[pltpu.VMEM((2, 2, tm, d), F32), pltpu.SemaphoreType.DMA((2,))],
    )
    return pl.pallas_call(
        kern,
        grid_spec=grid_spec,
        out_shape=jax.ShapeDtypeStruct((t, d), F32),
        compiler_params=_cparams(("arbitrary",)),
        name="combine",
    )(dest_flat, x2, rw, yb)


def _rope_slot_tables(seq):
    inv = 1.0 / (ROPE_BASE ** (jnp.arange(0, QK_ROPE_DIM, 2, dtype=F32) / QK_ROPE_DIM))
    ang = jnp.arange(seq, dtype=F32)[:, None] * inv[None, :]
    cos, sin = jnp.cos(ang), jnp.sin(ang)
    zero = jnp.zeros_like(cos)
    cos_t = jnp.concatenate([cos, cos, zero, zero], axis=-1)
    sin_a = jnp.concatenate([-sin, zero, zero, zero], axis=-1)
    sin_b = jnp.concatenate([zero, sin, zero, zero], axis=-1)
    return cos_t, sin_a, sin_b


def _pick_tile(n, pref):
    t = min(n, pref)
    while n % t:
        t //= 2
    return t


def kernel(x, mem, mix_norm_g, w_in, q_lora_norm_g, kv_lora_norm_g, w_uq, w_ukv, att_q_norm_g, att_k_norm_g, conv_w, conv_b, w_rgate, b_rgate, w_igate, b_igate, lru_lambda, att_out_norm_g, rnn_out_norm_g, w_out, xattn_norm_g, mem_norm_g, w_mq, w_mk, w_mv, mem_q_norm_g, mem_k_norm_g, w_mo, moe_norm_g, w_router_group, b_router_group, w_router_expert, b_router_expert, w_exp_gate, w_exp_up, w_exp_down):
    batch, seq, d = x.shape
    depth = w_in.shape[0]
    t = batch * seq
    d_att = N_ATT_HEADS * LANE
    d_rnn = conv_w.shape[-1]
    tm = _pick_tile(seq, 256)
    tq = _pick_tile(seq, 512)
    tc = _pick_tile(seq, 512)
    bm = 256
    n_asg = 2 * t
    n_rows = (-(-n_asg // bm) + N_EXPERTS) * bm
    n_blk = n_rows // bm

    cos_t, sin_a, sin_b = _rope_slot_tables(seq)
    xc = x.reshape(t, d)
    row = lambda a: a[:, None, :]

    o_pe = Q_LORA_RANK + KV_LORA_RANK + QK_ROPE_DIM
    w_att = jnp.concatenate([w_in[:, :, :o_pe], jnp.zeros((depth, d, LANE - QK_ROPE_DIM), F32)],
                            axis=2).astype(BF16)
    w_rnn = w_in[:, :, o_pe:].astype(BF16)
    wuq = jnp.pad(w_uq.reshape(depth, Q_LORA_RANK, N_ATT_HEADS, QK_HEAD_DIM),
                  ((0, 0), (0, 0), (0, 0), (0, HEAD_SLOT - QK_HEAD_DIM))
                  ).reshape(depth, Q_LORA_RANK, N_ATT_HEADS * HEAD_SLOT).astype(BF16)
    wukv = w_ukv.reshape(depth, KV_LORA_RANK, N_ATT_HEADS, 2 * LANE)
    wuk = wukv[..., :LANE].reshape(depth, KV_LORA_RANK, d_att).astype(BF16)
    wuv = wukv[..., LANE:].reshape(depth, KV_LORA_RANK, d_att).astype(BF16)
    slot_pad = jnp.zeros((depth, HEAD_SLOT - QK_HEAD_DIM), F32)
    qkv_params = (row(mix_norm_g), w_att, row(q_lora_norm_g), row(kv_lora_norm_g), wuq, wuk, wuv,
                  row(jnp.concatenate([att_q_norm_g, slot_pad], axis=1)) * (LOG2_E * QK_HEAD_DIM ** -0.5),
                  row(jnp.concatenate([att_k_norm_g, slot_pad], axis=1)))
    rnn_params = (row(mix_norm_g), w_rnn, conv_w, row(conv_b), w_rgate.astype(BF16), row(b_rgate),
                  w_igate.astype(BF16), row(b_igate), row(lru_lambda))
    mem_params = (row(mem_norm_g), w_mk.astype(BF16), w_mv.astype(BF16), row(mem_k_norm_g))
    n_pad = LANE - N_GROUPS - N_EXPERTS
    wrt = jnp.concatenate([w_router_group, w_router_expert, jnp.zeros((depth, d, n_pad), F32)], axis=2)
    brt = jnp.concatenate([b_router_group, b_router_expert, jnp.zeros((depth, n_pad), F32)], axis=1)
    wrt_hi = wrt.astype(BF16)
    wrt_lo = (wrt - wrt_hi.astype(F32)).astype(BF16)
    post_params = (row(att_out_norm_g), row(rnn_out_norm_g), w_out[:, :d_att].astype(BF16),
                   w_out[:, d_att:].astype(BF16), row(xattn_norm_g), w_mq.astype(BF16),
                   row(mem_q_norm_g), w_mo.astype(BF16), row(moe_norm_g), wrt_hi, wrt_lo, row(brt))

    for l in range(depth):
        q, k, v = _qkv(xc, qkv_params, l, cos_t, sin_a, sin_b, seq, tq)
        y_att = _attention(q.reshape(batch, seq, -1), k.reshape(batch, seq, -1),
                           v.reshape(batch, seq, -1), tc).reshape(t, d_att)
        y_rnn = _rglru(xc, rnn_params, l, batch, seq, tq)
        km, vm = _memkv(mem, mem_params, l)
        x2, h2, ri, rw, cnt = _post(xc, y_att, y_rnn, km, vm, post_params, l, seq, tm)

        counts = cnt[0, :N_EXPERTS].astype(jnp.int32)
        padded = (counts + bm - 1) // bm * bm
        pad_end = jnp.cumsum(padded)
        pad_start = pad_end - padded
        ri_t = ri[:, 0:4].T
        is_e = ri_t[0:2][None] == jnp.arange(N_EXPERTS, dtype=jnp.int32)[:, None, None]
        seg_start = jnp.sum(jnp.where(is_e, pad_start[:, None, None], 0), axis=0)
        dest = (seg_start + ri_t[2:4]).reshape(n_asg)
        blk_start = jnp.arange(n_blk, dtype=jnp.int32) * bm
        blk_e = jnp.minimum(jnp.sum((pad_end[None, :] <= blk_start[:, None]).astype(jnp.int32), axis=1),
                            N_EXPERTS - 1)
        nused = pad_end[-1:] // bm

        xb = _dispatch(dest, pad_end, h2, n_rows, tm, bm)
        yb = _experts(blk_e, nused, xb, w_exp_gate, w_exp_up, w_exp_down, l, bm)
        xc = _combine(dest, x2, rw, yb, tm)

    return xc.reshape(batch, seq, d)
```

```python
import functools

import jax
import jax.numpy as jnp
from jax import lax
from jax.experimental import pallas as pl
from jax.experimental.pallas import tpu as pltpu

EPS = 1e-6
N_ATT_HEADS = 8
QK_NOPE_DIM = 128
QK_ROPE_DIM = 64
QK_HEAD_DIM = QK_NOPE_DIM + QK_ROPE_DIM
Q_LORA_RANK = 512
KV_LORA_RANK = 256
ROPE_BASE = 10000.0
N_RNN_BLOCKS = 8
CONV_WIDTH = 4
LRU_C = 8.0
N_MEM_HEADS = 4
MEM_HEAD_DIM = 128
N_GROUPS = 4
EXPERTS_PER_GROUP = 8
N_EXPERTS = N_GROUPS * EXPERTS_PER_GROUP

LANE = 128
HEAD_SLOT = 2 * LANE
NEG_BIG = -1e30
LOG2_E = 1.4426950408889634
ATTN_UNROLL = 4

VMEM_LIMIT = 56 * 1024 * 1024

F32 = jnp.float32
BF16 = jnp.bfloat16


def _cparams(sem):
    return pltpu.CompilerParams(dimension_semantics=sem, vmem_limit_bytes=VMEM_LIMIT)


def _rms(x, g):
    return x * lax.rsqrt(jnp.mean(x * x, axis=-1, keepdims=True) + EPS) * g


def _rope_slot(r, cos_t, sin_a, sin_b):
    return r * cos_t + pltpu.roll(r, 96, 1) * sin_a + pltpu.roll(r, 32, 1) * sin_b


def _lspec(a, layer):
    return pl.BlockSpec((None,) + a.shape[1:], lambda *_: (layer,) + (0,) * (a.ndim - 1))


def _qkv_kernel(x_ref, gmix_ref, win_ref, gql_ref, gkvl_ref, wuq_ref, wuk_ref, wuv_ref,
                gq_ref, gk_ref, cos_ref, sina_ref, sinb_ref, q_ref, k_ref, v_ref):
    h = _rms(x_ref[...], gmix_ref[...]).astype(BF16)
    z = jnp.dot(h, win_ref[...], preferred_element_type=F32)
    o_kv = Q_LORA_RANK
    o_pe = o_kv + KV_LORA_RANK
    o_xr = o_pe + LANE

    cos_t = cos_ref[...]
    sin_a = sina_ref[...]
    sin_b = sinb_ref[...]
    inv_d = 1.0 / QK_HEAD_DIM

    cq = _rms(z[:, :o_kv], gql_ref[...]).astype(BF16)
    qf = jnp.dot(cq, wuq_ref[...], preferred_element_type=F32)
    gq = gq_ref[...]
    for hd in range(N_ATT_HEADS):
        slot = qf[:, hd * HEAD_SLOT:(hd + 1) * HEAD_SLOT]
        ss = jnp.sum(slot * slot, axis=-1, keepdims=True) * inv_d
        sn = slot * lax.rsqrt(ss + EPS) * gq
        q_ref[:, hd * HEAD_SLOT:hd * HEAD_SLOT + LANE] = sn[:, :LANE].astype(BF16)
        q_ref[:, hd * HEAD_SLOT + LANE:(hd + 1) * HEAD_SLOT] = _rope_slot(
            sn[:, LANE:], cos_t, sin_a, sin_b).astype(BF16)

    ckv = _rms(z[:, o_kv:o_pe], gkvl_ref[...]).astype(BF16)
    kn = jnp.dot(ckv, wuk_ref[...], preferred_element_type=F32)
    vv = jnp.dot(ckv, wuv_ref[...], preferred_element_type=F32).astype(BF16)
    ones = jnp.ones((vv.shape[0], LANE), BF16)
    for hd in range(N_ATT_HEADS):
        v_ref[:, hd * HEAD_SLOT:hd * HEAD_SLOT + LANE] = vv[:, hd * LANE:(hd + 1) * LANE]
        v_ref[:, hd * HEAD_SLOT + LANE:(hd + 1) * HEAD_SLOT] = ones
    kpe = z[:, o_pe:o_xr]
    pe_ss = jnp.sum(kpe * kpe, axis=-1, keepdims=True)
    gk = gk_ref[...]
    kpe_rot = _rope_slot(kpe * gk[:, LANE:], cos_t, sin_a, sin_b)
    for hd in range(N_ATT_HEADS):
        kh = kn[:, hd * LANE:(hd + 1) * LANE]
        ss = (jnp.sum(kh * kh, axis=-1, keepdims=True) + pe_ss) * inv_d
        inv = lax.rsqrt(ss + EPS)
        k_ref[:, hd * HEAD_SLOT:hd * HEAD_SLOT + LANE] = (kh * inv * gk[:, :LANE]).astype(BF16)
        k_ref[:, hd * HEAD_SLOT + LANE:(hd + 1) * HEAD_SLOT] = (kpe_rot * inv).astype(BF16)


def _qkv(x2d, params, layer, cos_t, sin_a, sin_b, seq, tm):
    t, d = x2d.shape
    n_s = seq // tm
    tile = lambda i: (i, 0)
    pos = lambda i: (i % n_s, 0)
    slots = N_ATT_HEADS * HEAD_SLOT
    return pl.pallas_call(
        _qkv_kernel,
        grid=(t // tm,),
        in_specs=[pl.BlockSpec((tm, d), tile)] + [_lspec(a, layer) for a in params]
                 + [pl.BlockSpec((tm, LANE), pos)] * 3,
        out_specs=[pl.BlockSpec((tm, slots), tile)] * 3,
        out_shape=[jax.ShapeDtypeStruct((t, slots), BF16)] * 3,
        compiler_params=_cparams(("parallel",)),
        name="qkv",
    )(x2d, *params, cos_t, sin_a, sin_b)


def _attn_pairs(nq):
    full = [(i, j) for i in range(nq) for j in range(i)]
    diag = [(i, i) for i in range(nq)]
    return full + diag, len(full)


def _attn_kernel(qi_ref, kj_ref, q_ref, k_ref, v_ref, o_ref, acc_all, m_all,
                 s0, s1, p0, p1, a0, a1, *, tc, n_pairs, n_full):
    m_all[...] = jnp.full(m_all.shape, NEG_BIG, F32)
    acc_all[...] = jnp.zeros(acc_all.shape, F32)
    sb, pb, ab = (s0, s1), (p0, p1), (a0, a1)

    def rows(idx):
        return pl.ds(pl.multiple_of(idx * tc, tc), tc)

    def stage_a(n, par):
        sb[par][...] = lax.dot_general(q_ref[rows(qi_ref[n]), :], k_ref[rows(kj_ref[n]), :],
                                       (((1,), (1,)), ((), ())), preferred_element_type=F32)

    def stage_b(n, par, masked):
        r = rows(qi_ref[n])
        s = sb[par][...]
        if masked:
            rpos = lax.broadcasted_iota(jnp.int32, (tc, tc), 0)
            cpos = lax.broadcasted_iota(jnp.int32, (tc, tc), 1)
            s = jnp.where(cpos <= rpos, s, NEG_BIG)
        m_prev = m_all[r, :]
        m_new = jnp.maximum(m_prev, jnp.max(s, axis=-1, keepdims=True))
        pb[par][...] = jnp.exp2(s - jnp.tile(m_new, (1, tc // LANE))).astype(BF16)
        ab[par][...] = jnp.exp2(m_prev - m_new)
        m_all[r, :] = m_new

    def stage_c(n, par):
        r = rows(qi_ref[n])
        pv = jnp.dot(pb[par][...], v_ref[rows(kj_ref[n]), :], preferred_element_type=F32)
        acc_all[r, :] = jnp.tile(ab[par][...], (1, 2)) * acc_all[r, :] + pv

    def sub(n, par, do_a, do_b, do_c, masked):
        if do_a:
            stage_a(n + 2, par)
        if do_b:
            stage_b(n + 1, 1 - par, masked)
        if do_c:
            stage_c(n, par)

    runs = []
    for n in range(-2, n_pairs):
        flags = (n + 2 < n_pairs, 0 <= n + 1 < n_pairs, n >= 0, n + 1 >= n_full)
        if runs and runs[-1][1] == flags:
            runs[-1][2] += 1
        else:
            runs.append([n, flags, 1])
    for n_start, flags, count in runs:
        par0 = n_start % 2
        n_loop = count // ATTN_UNROLL if all(flags[:3]) else 0
        if n_loop:
            def body(u, carry, n_start=n_start, flags=flags, par0=par0):
                n = n_start + ATTN_UNROLL * u
                for k in range(ATTN_UNROLL):
                    sub(n + k, (par0 + k) % 2, *flags)
                return carry

            lax.fori_loop(0, n_loop, body, 0)
        for n in range(n_start + ATTN_UNROLL * n_loop, n_start + count):
            sub(n, n % 2, *flags)

    for it in range(acc_all.shape[0] // tc):
        r = slice(it * tc, (it + 1) * tc)
        o_ref[r, :] = acc_all[r, :LANE] / acc_all[r, LANE:]


def _attention(q, k, v, tc):
    b, s, _ = q.shape
    nq = s // tc
    pairs, n_full = _attn_pairs(nq)
    qi = jnp.asarray([p[0] for p in pairs], jnp.int32)
    kj = jnp.asarray([p[1] for p in pairs], jnp.int32)
    kern = functools.partial(_attn_kernel, tc=tc, n_pairs=len(pairs), n_full=n_full)
    head_blk = lambda bi, h, qi_r, kj_r: (bi, 0, h)
    grid_spec = pltpu.PrefetchScalarGridSpec(
        num_scalar_prefetch=2,
        grid=(b, N_ATT_HEADS),
        in_specs=[pl.BlockSpec((None, s, HEAD_SLOT), head_blk),
                  pl.BlockSpec((None, s, HEAD_SLOT), head_blk),
                  pl.BlockSpec((None, s, HEAD_SLOT), head_blk)],
        out_specs=pl.BlockSpec((None, s, LANE), head_blk),
        scratch_shapes=[pltpu.VMEM((s, HEAD_SLOT), F32), pltpu.VMEM((s, LANE), F32),
                        pltpu.VMEM((tc, tc), F32), pltpu.VMEM((tc, tc), F32),
                        pltpu.VMEM((tc, tc), BF16), pltpu.VMEM((tc, tc), BF16),
                        pltpu.VMEM((tc, LANE), F32), pltpu.VMEM((tc, LANE), F32)],
    )
    return pl.pallas_call(
        kern,
        grid_spec=grid_spec,
        out_shape=jax.ShapeDtypeStruct((b, s, N_ATT_HEADS * LANE), F32),
        compiler_params=_cparams(("parallel", "parallel")),
        name="attn",
    )(qi, kj, q, k, v)


def _shift_rows(x, sh, before):
    n, c = x.shape
    rot = pltpu.roll(x.reshape(n // 8, 8, c), sh, 1)
    prev = jnp.concatenate([before[None], rot[:-1]], axis=0)
    row8 = lax.broadcasted_iota(jnp.int32, (1, 8, 1), 1)
    return jnp.where(row8 < sh, prev, rot).reshape(n, c)


def _rglru_kernel(x_ref, gmix_ref, win_ref, cw_ref, cb_ref, wr_ref, br_ref, wi_ref, bi_ref, lam_ref,
                  y_ref, tail_sc, h_sc, *, ts):
    i = pl.program_id(1)

    @pl.when(i == 0)
    def _():
        tail_sc[...] = jnp.zeros(tail_sc.shape, F32)
        h_sc[...] = jnp.zeros(h_sc.shape, F32)

    c = y_ref.shape[-1]
    hn = _rms(x_ref[...], gmix_ref[...]).astype(BF16)
    z = jnp.dot(hn, win_ref[...], preferred_element_type=F32)
    xr = z[:, :c]
    gate = z[:, c:]

    cw = cw_ref[...]
    tail = tail_sc[...]
    xc = cb_ref[...] + xr * cw[CONV_WIDTH - 1:CONV_WIDTH, :]
    for sh in range(1, CONV_WIDTH):
        shifted = _shift_rows(xr, sh, pltpu.roll(tail, sh, 0))
        xc = xc + shifted * cw[CONV_WIDTH - 1 - sh:CONV_WIDTH - sh, :]
    tail_sc[...] = xr[ts - 8:, :]

    blk = c // N_RNN_BLOCKS
    xcb = xc.astype(BF16)
    r_parts = []
    i_parts = []
    for n in range(N_RNN_BLOCKS):
        xb = xcb[:, n * blk:(n + 1) * blk]
        r_parts.append(jnp.dot(xb, wr_ref[n], preferred_element_type=F32))
        i_parts.append(jnp.dot(xb, wi_ref[n], preferred_element_type=F32))
    r = 0.5 * jnp.tanh(0.5 * (jnp.concatenate(r_parts, axis=-1) + br_ref[...])) + 0.5
    ig = 0.5 * jnp.tanh(0.5 * (jnp.concatenate(i_parts, axis=-1) + bi_ref[...])) + 0.5

    nlam = -lam_ref[...]
    softplus = jnp.maximum(nlam, 0.0) + jnp.log(1.0 + jnp.exp(-jnp.abs(nlam)))
    log_a = (-LRU_C * r) * softplus
    a = jnp.exp(log_a)
    u = jnp.sqrt(1.0 - a * a) * (ig * xc)

    ones8 = jnp.ones((8, c), F32)
    sh = 1
    while sh < ts:
        if sh < 8:
            u = a * _shift_rows(u, sh, jnp.zeros((8, c), F32)) + u
            a = a * _shift_rows(a, sh, ones8)
        else:
            u = jnp.concatenate([u[:sh], a[sh:] * u[:ts - sh] + u[sh:]], axis=0)
            a = jnp.concatenate([a[:sh], a[sh:] * a[:ts - sh]], axis=0)
        sh *= 2
    hcur = u + a * h_sc[0:1, :]
    h_sc[...] = jnp.broadcast_to(hcur[ts - 1:ts, :], h_sc.shape)
    y_ref[...] = hcur * jax.nn.gelu(gate)


def _rglru(x2d, params, layer, batch, seq, ts):
    t, d = x2d.shape
    c = params[2].shape[-1]
    n_s = seq // ts
    tile = lambda b, i: (b * n_s + i, 0)
    kern = functools.partial(_rglru_kernel, ts=ts)
    return pl.pallas_call(
        kern,
        grid=(batch, n_s),
        in_specs=[pl.BlockSpec((ts, d), tile)] + [_lspec(a, layer) for a in params],
        out_specs=pl.BlockSpec((ts, c), tile),
        out_shape=jax.ShapeDtypeStruct((t, c), F32),
        scratch_shapes=[pltpu.VMEM((8, c), F32), pltpu.VMEM((8, c), F32)],
        compiler_params=_cparams(("parallel", "arbitrary")),
        name="rglru",
    )(x2d, *params)


def _memkv_kernel(mem_ref, g_ref, wk_ref, wv_ref, gk_ref, k_ref, v_ref):
    m = _rms(mem_ref[...], g_ref[...]).astype(BF16)
    kf = jnp.dot(m, wk_ref[...], preferred_element_type=F32)
    v_ref[...] = jnp.dot(m, wv_ref[...], preferred_element_type=F32).astype(BF16)
    gk = gk_ref[...]
    for hd in range(N_MEM_HEADS):
        sl = slice(hd * MEM_HEAD_DIM, (hd + 1) * MEM_HEAD_DIM)
        k_ref[:, sl] = _rms(kf[:, sl], gk).astype(BF16)


def _memkv(mem, params, layer):
    b, m_len, d = mem.shape
    dk = params[1].shape[-1]
    return pl.pallas_call(
        _memkv_kernel,
        grid=(b,),
        in_specs=[pl.BlockSpec((None, m_len, d), lambda i: (i, 0, 0))]
                 + [_lspec(a, layer) for a in params],
        out_specs=[pl.BlockSpec((None, m_len, dk), lambda i: (i, 0, 0)),
                   pl.BlockSpec((None, m_len, dk), lambda i: (i, 0, 0))],
        out_shape=[jax.ShapeDtypeStruct((b, m_len, dk), BF16),
                   jax.ShapeDtypeStruct((b, m_len, dk), BF16)],
        compiler_params=_cparams(("parallel",)),
        name="memkv",
    )(mem, *params)


def _post_kernel(x_ref, ya_ref, yr_ref, km_ref, vm_ref, ga_ref, gr_ref, woa_ref, wor_ref, gx_ref,
                 wmq_ref, gmq_ref, wmo_ref, gmoe_ref, wrt_ref, wrl_ref, brt_ref,
                 x2_ref, h2_ref, ri_ref, rw_ref, cnt_ref, cnt_sc, *, tm):
    step = pl.program_id(0)

    @pl.when(step == 0)
    def _():
        cnt_sc[...] = jnp.zeros(cnt_sc.shape, F32)

    ya = _rms(ya_ref[...], ga_ref[...]).astype(BF16)
    yr = _rms(yr_ref[...], gr_ref[...]).astype(BF16)
    x1 = (x_ref[...] + jnp.dot(ya, woa_ref[...], preferred_element_type=F32)
          + jnp.dot(yr, wor_ref[...], preferred_element_type=F32))

    hq = _rms(x1, gx_ref[...]).astype(BF16)
    qf = jnp.dot(hq, wmq_ref[...], preferred_element_type=F32)
    gmq = gmq_ref[...]
    o_parts = []
    for hd in range(N_MEM_HEADS):
        sl = slice(hd * MEM_HEAD_DIM, (hd + 1) * MEM_HEAD_DIM)
        qh = (_rms(qf[:, sl], gmq) * (MEM_HEAD_DIM ** -0.5)).astype(BF16)
        s = lax.dot_general(qh, km_ref[:, sl], (((1,), (1,)), ((), ())), preferred_element_type=F32)
        p = jnp.exp(s - jnp.max(s, axis=-1, keepdims=True))
        p = p / jnp.sum(p, axis=-1, keepdims=True)
        o_parts.append(jnp.dot(p.astype(BF16), vm_ref[:, sl], preferred_element_type=F32))
    o = jnp.concatenate(o_parts, axis=-1).astype(BF16)
    x2 = x1 + jnp.dot(o, wmo_ref[...], preferred_element_type=F32)
    x2_ref[...] = x2

    h2 = _rms(x2, gmoe_ref[...])
    h2_ref[...] = h2
    h_hi = h2.astype(BF16)
    h_lo = (h2 - h_hi.astype(F32)).astype(BF16)
    w_hi = wrt_ref[...]
    logits = (jnp.dot(h_hi, w_hi, preferred_element_type=F32)
              + jnp.dot(h_lo, w_hi, preferred_element_type=F32)
              + jnp.dot(h_hi, wrl_ref[...], preferred_element_type=F32)) + brt_ref[...]
    lane_i = lax.broadcasted_iota(jnp.int32, (tm, LANE), 1)
    lane = lane_i.astype(F32)
    big = float(LANE)
    is_g = lane_i < N_GROUPS
    gl = jnp.where(is_g, logits, NEG_BIG)
    gmax = jnp.max(gl, axis=-1, keepdims=True)
    gidx = jnp.min(jnp.where(gl == gmax, lane, big), axis=-1, keepdims=True)
    g_w = 1.0 / jnp.sum(jnp.where(is_g, jnp.exp(gl - gmax), 0.0), axis=-1, keepdims=True)
    lo = N_GROUPS + gidx * EXPERTS_PER_GROUP
    in_grp = (lane >= lo) & (lane < lo + EXPERTS_PER_GROUP)
    el = jnp.where(in_grp, logits, NEG_BIG)
    m1 = jnp.max(el, axis=-1, keepdims=True)
    i1 = jnp.min(jnp.where(el == m1, lane, big), axis=-1, keepdims=True)
    el2 = jnp.where(lane == i1, NEG_BIG, el)
    m2 = jnp.max(el2, axis=-1, keepdims=True)
    i2 = jnp.min(jnp.where(el2 == m2, lane, big), axis=-1, keepdims=True)
    zsum = jnp.sum(jnp.where(in_grp, jnp.exp(el - m1), 0.0), axis=-1, keepdims=True)
    p1 = 1.0 / zsum
    p2 = jnp.exp(m2 - m1) / zsum
    w1 = g_w * (p1 / (p1 + p2))
    w2 = g_w * (p2 / (p1 + p2))
    e1 = i1 - N_GROUPS
    e2 = i2 - N_GROUPS

    oh1 = lane == e1
    oh2 = lane == e2
    oh = jnp.where(oh1 | oh2, 1.0, 0.0)
    r_i = lax.broadcasted_iota(jnp.int32, (tm, tm), 0)
    c_i = lax.broadcasted_iota(jnp.int32, (tm, tm), 1)
    ltri = jnp.where(c_i < r_i, 1.0, 0.0).astype(BF16)
    cum = jnp.dot(ltri, oh.astype(BF16), preferred_element_type=F32) + cnt_sc[0:1, :]
    rank1 = jnp.sum(jnp.where(oh1, cum, 0.0), axis=-1, keepdims=True)
    rank2 = jnp.sum(jnp.where(oh2, cum, 0.0), axis=-1, keepdims=True)
    new_cnt = cnt_sc[0:1, :] + jnp.sum(oh, axis=0, keepdims=True)
    cnt_sc[...] = jnp.broadcast_to(new_cnt, cnt_sc.shape)
    cnt_ref[...] = jnp.broadcast_to(new_cnt, cnt_ref.shape)

    ri = jnp.where(lane_i == 0, e1, jnp.where(lane_i == 1, e2,
                   jnp.where(lane_i == 2, rank1, jnp.where(lane_i == 3, rank2, 0.0))))
    ri_ref[...] = ri.astype(jnp.int32)
    rw_ref[...] = jnp.where(lane_i == 0, w1, jnp.where(lane_i == 1, w2, 0.0))


def _post(x2d, ya, yr, km, vm, params, layer, seq, tm):
    t, d = x2d.shape
    n_s = seq // tm
    tile = lambda i: (i, 0)
    c2 = lambda i: (0, 0)
    mem_spec = lambda a: pl.BlockSpec((None,) + a.shape[1:], lambda i: (i // n_s, 0, 0))
    kern = functools.partial(_post_kernel, tm=tm)
    return pl.pallas_call(
        kern,
        grid=(t // tm,),
        in_specs=[pl.BlockSpec((tm, d), tile), pl.BlockSpec((tm, ya.shape[1]), tile),
                  pl.BlockSpec((tm, yr.shape[1]), tile), mem_spec(km), mem_spec(vm)]
                 + [_lspec(a, layer) for a in params],
        out_specs=[pl.BlockSpec((tm, d), tile), pl.BlockSpec((tm, d), tile),
                   pl.BlockSpec((tm, LANE), tile), pl.BlockSpec((tm, LANE), tile),
                   pl.BlockSpec((8, LANE), c2)],
        out_shape=[jax.ShapeDtypeStruct((t, d), F32), jax.ShapeDtypeStruct((t, d), F32),
                   jax.ShapeDtypeStruct((t, LANE), jnp.int32),
                   jax.ShapeDtypeStruct((t, LANE), F32),
                   jax.ShapeDtypeStruct((8, LANE), F32)],
        scratch_shapes=[pltpu.VMEM((8, LANE), F32)],
        compiler_params=_cparams(("arbitrary",)),
        name="post",
    )(x2d, ya, yr, km, vm, *params)


def _row_copy(src, dst, sem, src_row, dst_row):
    return pltpu.make_async_copy(src.at[pl.ds(src_row, 1)], dst.at[pl.ds(dst_row, 1)], sem)


def _dispatch_kernel(dest_ref, pend_ref, h_hbm, xb_hbm, zero_sc, stage, sem, lsem, zsem, *, tm, bm):
    i = pl.program_id(0)

    @pl.when(i == 0)
    def _():
        zero_sc[...] = jnp.zeros(zero_sc.shape, F32)

        def zero_copy(e):
            end = pend_ref[e]
            start = jnp.where(e == 0, 0, pend_ref[jnp.maximum(e - 1, 0)])
            dst = xb_hbm.at[pl.ds(pl.multiple_of(jnp.maximum(end - bm, 0), bm), bm)]
            return end > start, pltpu.make_async_copy(zero_sc, dst, zsem)

        def z_issue(e, c):
            nonempty, cp = zero_copy(e)

            @pl.when(nonempty)
            def _():
                cp.start()
            return c

        def z_drain(e, c):
            nonempty, cp = zero_copy(e)

            @pl.when(nonempty)
            def _():
                cp.wait()
            return c

        def tail_copy(b):
            dst = xb_hbm.at[pl.ds(pl.multiple_of(b * bm, bm), bm)]
            return pltpu.make_async_copy(zero_sc, dst, zsem)

        def t_issue(b, c):
            tail_copy(b).start()
            return c

        def t_drain(b, c):
            tail_copy(b).wait()
            return c

        first_unused = pend_ref[N_EXPERTS - 1] // bm
        n_blk = xb_hbm.shape[0] // bm
        lax.fori_loop(0, N_EXPERTS, z_issue, 0)
        lax.fori_loop(first_unused, n_blk, t_issue, 0)
        lax.fori_loop(0, N_EXPERTS, z_drain, 0)
        lax.fori_loop(first_unused, n_blk, t_drain, 0)

    n_steps = pl.num_programs(0)
    slot = i % 3

    def load(tile, to_slot):
        src = h_hbm.at[pl.ds(pl.multiple_of(tile * tm, tm), tm)]
        return pltpu.make_async_copy(src, stage.at[to_slot], lsem.at[to_slot])

    def wait_scatter(of_slot):
        for _ in range(2):
            pltpu.make_async_copy(stage.at[of_slot], xb_hbm.at[pl.ds(0, tm)], sem.at[of_slot]).wait()

    @pl.when(i == 0)
    def _():
        load(0, 0).start()

    @pl.when(i + 1 < n_steps)
    def _():
        load(i + 1, (i + 1) % 3).start()

    load(i, slot).wait()
    n_tok = tm * n_steps
    base = tm * i
    src = stage.at[slot]
    for r in range(tm):
        _row_copy(src, xb_hbm, sem.at[slot], r, dest_ref[base + r]).start()
        _row_copy(src, xb_hbm, sem.at[slot], r, dest_ref[n_tok + base + r]).start()

    @pl.when(i > 0)
    def _():
        wait_scatter((i + 2) % 3)

    @pl.when(i == n_steps - 1)
    def _():
        wait_scatter(slot)


def _dispatch(dest_flat, pad_end, h2, n_rows, tm, bm):
    t, d = h2.shape
    kern = functools.partial(_dispatch_kernel, tm=tm, bm=bm)
    grid_spec = pltpu.PrefetchScalarGridSpec(
        num_scalar_prefetch=2,
        grid=(t // tm,),
        in_specs=[pl.BlockSpec(memory_space=pl.ANY)],
        out_specs=pl.BlockSpec(memory_space=pl.ANY),
        scratch_shapes=[pltpu.VMEM((bm, d), F32), pltpu.VMEM((3, tm, d), F32),
                        pltpu.SemaphoreType.DMA((3,)), pltpu.SemaphoreType.DMA((3,)),
                        pltpu.SemaphoreType.DMA(())],
    )
    return pl.pallas_call(
        kern,
        grid_spec=grid_spec,
        out_shape=jax.ShapeDtypeStruct((n_rows, d), F32),
        compiler_params=_cparams(("arbitrary",)),
        name="dispatch",
    )(dest_flat, pad_end, h2)


def _expert_kernel(blk_e_ref, nused_ref, x_ref, wg_ref, wu_ref, wd_ref, y_ref,
                   wg_sc, wu_sc, wd_sc):
    i = pl.program_id(0)
    used = i < nused_ref[0]

    @pl.when(used)
    def _():
        prev = jnp.maximum(i - 1, 0)

        @pl.when((i == 0) | (blk_e_ref[i] != blk_e_ref[prev]))
        def _():
            wg_sc[...] = wg_ref[...].astype(BF16)
            wu_sc[...] = wu_ref[...].astype(BF16)
            wd_sc[...] = wd_ref[...].astype(BF16)

        x = x_ref[...].astype(BF16)
        g = jnp.dot(x, wg_sc[...], preferred_element_type=F32)
        u = jnp.dot(x, wu_sc[...], preferred_element_type=F32)
        act = (g * jax.nn.sigmoid(g) * u).astype(BF16)
        y_ref[...] = jnp.dot(act, wd_sc[...], preferred_element_type=F32)

    @pl.when(jnp.logical_not(used))
    def _():
        y_ref[...] = jnp.zeros(y_ref.shape, F32)


def _experts(blk_e, nused, xb, w_gate, w_up, w_down, layer, bm):
    n_rows, d = xb.shape
    de = w_gate.shape[-1]
    wmap = lambda i, be, nu: (layer, be[i], 0, 0)
    grid_spec = pltpu.PrefetchScalarGridSpec(
        num_scalar_prefetch=2,
        grid=(n_rows // bm,),
        in_specs=[pl.BlockSpec((bm, d), lambda i, be, nu: (jnp.minimum(i, nu[0] - 1), 0)),
                  pl.BlockSpec((None, None, d, de), wmap),
                  pl.BlockSpec((None, None, d, de), wmap),
                  pl.BlockSpec((None, None, de, d), wmap)],
        out_specs=pl.BlockSpec((bm, d), lambda i, be, nu: (i, 0)),
        scratch_shapes=[pltpu.VMEM((d, de), BF16), pltpu.VMEM((d, de), BF16),
                        pltpu.VMEM((de, d), BF16)],
    )
    return pl.pallas_call(
        _expert_kernel,
        grid_spec=grid_spec,
        out_shape=jax.ShapeDtypeStruct((n_rows, d), F32),
        compiler_params=_cparams(("arbitrary",)),
        name="experts",
    )(blk_e, nused, xb, w_gate, w_up, w_down)


def _combine_kernel(dest_ref, x_ref, rw_ref, y_hbm, o_ref, buf, sem, *, tm):
    i = pl.program_id(0)
    slot = i % 2

    def gather(tile, to_slot):
        n_tok = tm * pl.num_programs(0)
        base = tm * tile
        for r in range(tm):
            for k in range(2):
                _row_copy(y_hbm, buf.at[to_slot, k], sem.at[to_slot],
                          dest_ref[k * n_tok + base + r], r).start()

    @pl.when(i == 0)
    def _():
        gather(0, 0)

    @pl.when(i + 1 < pl.num_programs(0))
    def _():
        gather(i + 1, 1 - slot)

    for k in range(2):
        pltpu.make_async_copy(y_hbm.at[pl.ds(0, tm)], buf.at[slot, k], sem.at[slot]).wait()
    rw = rw_ref[...]
    o_ref[...] = x_ref[...] + rw[:, 0:1] * buf[slot, 0] + rw[:, 1:2] * buf[slot, 1]


def _combine(dest_flat, x2, rw, yb, tm):
    t, d = x2.shape
    kern = functools.partial(_combine_kernel, tm=tm)
    grid_spec = pltpu.PrefetchScalarGridSpec(
        num_scalar_prefetch=1,
        grid=(t // tm,),
        in_specs=[pl.BlockSpec((tm, d), lambda i, de: (i, 0)),
                  pl.BlockSpec((tm, LANE), lambda i, de: (i, 0)),
                  pl.BlockSpec(memory_space=pl.ANY)],
        out_specs=pl.BlockSpec((tm, d), lambda i, de: (i, 0)),
        scratch_shapes=[pltpu.VMEM((2, 2, tm, d), F32), pltpu.SemaphoreType.DMA((2,))],
    )
    return pl.pallas_call(
        kern,
        grid_spec=grid_spec,
        out_shape=jax.ShapeDtypeStruct((t, d), F32),
        compiler_params=_cparams(("arbitrary",)),
        name="combine",
    )(dest_flat, x2, rw, yb)


def _rope_slot_tables(seq):
    inv = 1.0 / (ROPE_BASE ** (jnp.arange(0, QK_ROPE_DIM, 2, dtype=F32) / QK_ROPE_DIM))
    ang = jnp.arange(seq, dtype=F32)[:, None] * inv[None, :]
    cos, sin = jnp.cos(ang), jnp.sin(ang)
    zero = jnp.zeros_like(cos)
    cos_t = jnp.concatenate([cos, cos, zero, zero], axis=-1)
    sin_a = jnp.concatenate([-sin, zero, zero, zero], axis=-1)
    sin_b = jnp.concatenate([zero, sin, zero, zero], axis=-1)
    return cos_t, sin_a, sin_b


def _pick_tile(n, pref):
    t = min(n, pref)
    while n % t:
        t //= 2
    return t


def kernel(x, mem, mix_norm_g, w_in, q_lora_norm_g, kv_lora_norm_g, w_uq, w_ukv, att_q_norm_g, att_k_norm_g, conv_w, conv_b, w_rgate, b_rgate, w_igate, b_igate, lru_lambda, att_out_norm_g, rnn_out_norm_g, w_out, xattn_norm_g, mem_norm_g, w_mq, w_mk, w_mv, mem_q_norm_g, mem_k_norm_g, w_mo, moe_norm_g, w_router_group, b_router_group, w_router_expert, b_router_expert, w_exp_gate, w_exp_up, w_exp_down):
    batch, seq, d = x.shape
    depth = w_in.shape[0]
    t = batch * seq
    d_att = N_ATT_HEADS * LANE
    d_rnn = conv_w.shape[-1]
    tm = _pick_tile(seq, 256)
    tq = _pick_tile(seq, 512)
    tc = _pick_tile(seq, 512)
    bm = 256
    n_asg = 2 * t
    n_rows = (-(-n_asg // bm) + N_EXPERTS) * bm
    n_blk = n_rows // bm

    cos_t, sin_a, sin_b = _rope_slot_tables(seq)
    xc = x.reshape(t, d)
    row = lambda a: a[:, None, :]

    o_pe = Q_LORA_RANK + KV_LORA_RANK + QK_ROPE_DIM
    w_att = jnp.concatenate([w_in[:, :, :o_pe], jnp.zeros((depth, d, LANE - QK_ROPE_DIM), F32)],
                            axis=2).astype(BF16)
    w_rnn = w_in[:, :, o_pe:].astype(BF16)
    wuq = jnp.pad(w_uq.reshape(depth, Q_LORA_RANK, N_ATT_HEADS, QK_HEAD_DIM),
                  ((0, 0), (0, 0), (0, 0), (0, HEAD_SLOT - QK_HEAD_DIM))
                  ).reshape(depth, Q_LORA_RANK, N_ATT_HEADS * HEAD_SLOT).astype(BF16)
    wukv = w_ukv.reshape(depth, KV_LORA_RANK, N_ATT_HEADS, 2 * LANE)
    wuk = wukv[..., :LANE].reshape(depth, KV_LORA_RANK, d_att).astype(BF16)
    wuv = wukv[..., LANE:].reshape(depth, KV_LORA_RANK, d_att).astype(BF16)
    slot_pad = jnp.zeros((depth, HEAD_SLOT - QK_HEAD_DIM), F32)
    qkv_params = (row(mix_norm_g), w_att, row(q_lora_norm_g), row(kv_lora_norm_g), wuq, wuk, wuv,
                  row(jnp.concatenate([att_q_norm_g, slot_pad], axis=1)) * (LOG2_E * QK_HEAD_DIM ** -0.5),
                  row(jnp.concatenate([att_k_norm_g, slot_pad], axis=1)))
    rnn_params = (row(mix_norm_g), w_rnn, conv_w, row(conv_b), w_rgate.astype(BF16), row(b_rgate),
                  w_igate.astype(BF16), row(b_igate), row(lru_lambda))
    mem_params = (row(mem_norm_g), w_mk.astype(BF16), w_mv.astype(BF16), row(mem_k_norm_g))
    n_pad = LANE - N_GROUPS - N_EXPERTS
    wrt = jnp.concatenate([w_router_group, w_router_expert, jnp.zeros((depth, d, n_pad), F32)], axis=2)
    brt = jnp.concatenate([b_router_group, b_router_expert, jnp.zeros((depth, n_pad), F32)], axis=1)
    wrt_hi = wrt.astype(BF16)
    wrt_lo = (wrt - wrt_hi.astype(F32)).astype(BF16)
    post_params = (row(att_out_norm_g), row(rnn_out_norm_g), w_out[:, :d_att].astype(BF16),
                   w_out[:, d_att:].astype(BF16), row(xattn_norm_g), w_mq.astype(BF16),
                   row(mem_q_norm_g), w_mo.astype(BF16), row(moe_norm_g), wrt_hi, wrt_lo, row(brt))

    for l in range(depth):
        q, k, v = _qkv(xc, qkv_params, l, cos_t, sin_a, sin_b, seq, tq)
        y_att = _attention(q.reshape(batch, seq, -1), k.reshape(batch, seq, -1),
                           v.reshape(batch, seq, -1), tc).reshape(t, d_att)
        y_rnn = _rglru(xc, rnn_params, l, batch, seq, tq)
        km, vm = _memkv(mem, mem_params, l)
        x2, h2, ri, rw, cnt = _post(xc, y_att, y_rnn, km, vm, post_params, l, seq, tm)

        counts = cnt[0, :N_EXPERTS].astype(jnp.int32)
        padded = (counts + bm - 1) // bm * bm
        pad_end = jnp.cumsum(padded)
        pad_start = pad_end - padded
        ri_t = ri[:, 0:4].T
        is_e = ri_t[0:2][None] == jnp.arange(N_EXPERTS, dtype=jnp.int32)[:, None, None]
        seg_start = jnp.sum(jnp.where(is_e, pad_start[:, None, None], 0), axis=0)
        dest = (seg_start + ri_t[2:4]).reshape(n_asg)
        blk_start = jnp.arange(n_blk, dtype=jnp.int32) * bm
        blk_e = jnp.minimum(jnp.sum((pad_end[None, :] <= blk_start[:, None]).astype(jnp.int32), axis=1),
                            N_EXPERTS - 1)
        nused = pad_end[-1:] // bm

        xb = _dispatch(dest, pad_end, h2, n_rows, tm, bm)
        yb = _experts(blk_e, nused, xb, w_exp_gate, w_exp_up, w_exp_down, l, bm)
        xc = _combine(dest, x2, rw, yb, tm)

    return xc.reshape(batch, seq, d)
```
